```python
import math
import jax, jax.numpy as jnp
from jax import lax
import numpy as np

D_MODEL = 1024
BATCH = 4
SEQ = 4096
DEPTH = 2

D_MIX = D_MODEL
S5_WIDTH = D_MIX // 2
S5_GROUP = 16
S5_GROUPS = S5_WIDTH // S5_GROUP
S5_STATE = 64
DT_MIN = 0.001
DT_MAX = 0.1
SB_WIDTH = D_MIX - S5_WIDTH
SB_HEAD_DIM = 64
SB_HEADS = SB_WIDTH // SB_HEAD_DIM
SB_BLOCK = 128
CONV_CH = D_MODEL
CONV_K = 3
D_FF = 2752
N_EVEN = (DEPTH + 1) // 2
N_ODD = DEPTH // 2
EPS = 1e-6

kernel_name = "hybrid_s5_stickbreak_shortconv_macaron"


def rmsnorm(x, g):
    xf = x.astype(jnp.float32)
    r = lax.rsqrt(jnp.mean(xf * xf, axis=-1, keepdims=True) + EPS)
    return (xf * r * g.astype(jnp.float32)).astype(x.dtype)


def swiglu(h, w_gate, w_up, w_down):
    return (jax.nn.silu(h @ w_gate) * (h @ w_up)) @ w_down


def _complex_linear_combine(e1, e2):
    a1r, a1i, b1r, b1i = e1
    a2r, a2i, b2r, b2i = e2
    return (a2r * a1r - a2i * a1i,
            a2r * a1i + a2i * a1r,
            a2r * b1r - a2i * b1i + b2r,
            a2r * b1i + a2i * b1r + b2i)


def s5_mixer(u, lam_re, lam_im, log_dt, b_re, b_im, c_re, c_im, d, w_glu):
    bsz, seq, _ = u.shape
    uf = u.astype(jnp.float32).reshape(bsz, seq, S5_GROUPS, S5_GROUP)
    lr = lam_re.astype(jnp.float32)
    li = lam_im.astype(jnp.float32)
    dt = jnp.exp(log_dt.astype(jnp.float32))[:, None]
    mag = jnp.exp(lr * dt)
    ab_re = mag * jnp.cos(li * dt)
    ab_im = mag * jnp.sin(li * dt)
    den = lr * lr + li * li
    nr = ab_re - 1.0
    coef_re = (nr * lr + ab_im * li) / den
    coef_im = (ab_im * lr - nr * li) / den
    br = b_re.astype(jnp.float32)
    bi = b_im.astype(jnp.float32)
    bb_re = coef_re[..., None] * br - coef_im[..., None] * bi
    bb_im = coef_re[..., None] * bi + coef_im[..., None] * br
    bu_re = jnp.einsum('blgp,gnp->blgn', uf, bb_re)
    bu_im = jnp.einsum('blgp,gnp->blgn', uf, bb_im)
    a_re = jnp.broadcast_to(ab_re, (seq,) + ab_re.shape)[None]
    a_im = jnp.broadcast_to(ab_im, (seq,) + ab_im.shape)[None]
    _, _, h_re, h_im = lax.associative_scan(
        _complex_linear_combine, (a_re, a_im, bu_re, bu_im), axis=1)
    y = (jnp.einsum('blgn,gpn->blgp', h_re, c_re.astype(jnp.float32))
         - jnp.einsum('blgn,gpn->blgp', h_im, c_im.astype(jnp.float32))
         + d.astype(jnp.float32).reshape(S5_GROUPS, S5_GROUP) * uf)
    y = y.reshape(bsz, seq, S5_WIDTH)
    z = jax.nn.gelu(y)
    out = z * jax.nn.sigmoid(z @ w_glu.astype(jnp.float32))
    return out.astype(u.dtype)


def stick_breaking_attention(q, k, v):
    bsz, seq, nh, dh = q.shape
    qh = q.transpose(0, 2, 1, 3).astype(jnp.float32)
    kh = k.transpose(0, 2, 1, 3).astype(jnp.float32)
    vh = v.transpose(0, 2, 1, 3)
    scale = 1.0 / math.sqrt(dh)
    key_pos = jnp.arange(seq)
    n_blocks = seq // SB_BLOCK

    def block(i):
        start = i * SB_BLOCK
        qb = lax.dynamic_slice_in_dim(qh, start, SB_BLOCK, axis=2)
        z = jnp.einsum('bhqd,bhkd->bhqk', qb, kh) * scale
        q_pos = start + jnp.arange(SB_BLOCK)
        mask = key_pos[None, :] < q_pos[:, None]
        log_keep = jnp.where(mask, jax.nn.log_sigmoid(-z), 0.0)
        later = lax.cumsum(log_keep, axis=3, reverse=True) - log_keep
        w = jnp.where(mask, jnp.exp(jax.nn.log_sigmoid(z) + later), 0.0)
        return jnp.einsum('bhqk,bhkd->bhqd', w.astype(vh.dtype), vh)

    out = lax.map(block, jnp.arange(n_blocks))
    out = out.transpose(1, 0, 3, 2, 4).reshape(bsz, seq, nh * dh)
    return out


def parallel_s5_stickbreak(h, w_in, lam_re, lam_im, log_dt, b_re, b_im, c_re, c_im, d, w_glu, w_out):
    bsz, seq, _ = h.shape
    proj = h @ w_in
    u = proj[..., :S5_WIDTH]
    qkv = proj[..., S5_WIDTH:].reshape(bsz, seq, 3, SB_HEADS, SB_HEAD_DIM)
    y_a = s5_mixer(u, lam_re, lam_im, log_dt, b_re, b_im, c_re, c_im, d, w_glu)
    y_b = stick_breaking_attention(qkv[:, :, 0], qkv[:, :, 1], qkv[:, :, 2]).astype(y_a.dtype)
    return jnp.concatenate([y_a, y_b], axis=-1) @ w_out


def short_conv_mixer(h, w_in, conv_w, w_out):
    proj = h @ w_in
    b_gate, c_gate, v = jnp.split(proj, 3, axis=-1)
    y = lax.conv_general_dilated(
        c_gate * v, conv_w[:, None, :], window_strides=(1,),
        padding=[(CONV_K - 1, 0)], dimension_numbers=('NWC', 'WIO', 'NWC'),
        feature_group_count=CONV_CH)
    return (b_gate * y) @ w_out


def setup_inputs(seed: int = 0) -> dict:
    key = jax.random.key(seed)
    ks = jax.random.split(key, 32)
    nrm = jax.random.normal
    f32 = jnp.float32
    D, F = D_MODEL, D_FF
    G, N, P = S5_GROUPS, S5_STATE, S5_GROUP
    inp = {}
    inp['x'] = nrm(ks[0], (BATCH, SEQ, D), f32)
    inp['ffn1_norm'] = 1.0 + 0.02 * nrm(ks[1], (DEPTH, D), f32)
    inp['ffn1_w_gate'] = nrm(ks[2], (DEPTH, D, F), f32) * D ** -0.5
    inp['ffn1_w_up'] = nrm(ks[3], (DEPTH, D, F), f32) * D ** -0.5
    inp['ffn1_w_down'] = nrm(ks[4], (DEPTH, F, D), f32) * F ** -0.5
    inp['mix_norm'] = 1.0 + 0.02 * nrm(ks[5], (DEPTH, D), f32)
    inp['ffn2_norm'] = 1.0 + 0.02 * nrm(ks[6], (DEPTH, D), f32)
    inp['ffn2_w_gate'] = nrm(ks[7], (DEPTH, D, F), f32) * D ** -0.5
    inp['ffn2_w_up'] = nrm(ks[8], (DEPTH, D, F), f32) * D ** -0.5
    inp['ffn2_w_down'] = nrm(ks[9], (DEPTH, F, D), f32) * F ** -0.5
    inp['ab_w_in'] = nrm(ks[10], (N_EVEN, D, S5_WIDTH + 3 * SB_WIDTH), f32) * D ** -0.5
    inp['s5_lambda_re'] = -0.5 + 0.01 * nrm(ks[11], (N_EVEN, G, N), f32)
    inp['s5_lambda_im'] = jnp.broadcast_to(math.pi * jnp.arange(N, dtype=f32), (N_EVEN, G, N)) \
        + 0.01 * nrm(ks[12], (N_EVEN, G, N), f32)
    inp['s5_log_dt'] = jax.random.uniform(ks[13], (N_EVEN, G), f32, math.log(DT_MIN), math.log(DT_MAX))
    inp['s5_b_re'] = nrm(ks[14], (N_EVEN, G, N, P), f32) * (2 * P) ** -0.5
    inp['s5_b_im'] = nrm(ks[15], (N_EVEN, G, N, P), f32) * (2 * P) ** -0.5
    inp['s5_c_re'] = nrm(ks[16], (N_EVEN, G, P, N), f32) * N ** -0.5
    inp['s5_c_im'] = nrm(ks[17], (N_EVEN, G, P, N), f32) * N ** -0.5
    inp['s5_d'] = nrm(ks[18], (N_EVEN, S5_WIDTH), f32)
    inp['s5_w_glu'] = nrm(ks[19], (N_EVEN, S5_WIDTH, S5_WIDTH), f32) * S5_WIDTH ** -0.5
    inp['ab_w_out'] = nrm(ks[20], (N_EVEN, D_MIX, D), f32) * D_MIX ** -0.5
    inp['sc_w_in'] = nrm(ks[21], (N_ODD, D, 3 * CONV_CH), f32) * D ** -0.5
    inp['sc_conv_w'] = nrm(ks[22], (N_ODD, CONV_K, CONV_CH), f32) * CONV_K ** -0.5
    inp['sc_w_out'] = nrm(ks[23], (N_ODD, CONV_CH, D), f32) * CONV_CH ** -0.5
    inp['final_norm'] = 1.0 + 0.02 * nrm(ks[24], (D,), f32)
    return inp


def reference(x, ffn1_norm, ffn1_w_gate, ffn1_w_up, ffn1_w_down, mix_norm,
              ffn2_norm, ffn2_w_gate, ffn2_w_up, ffn2_w_down,
              ab_w_in, s5_lambda_re, s5_lambda_im, s5_log_dt, s5_b_re, s5_b_im,
              s5_c_re, s5_c_im, s5_d, s5_w_glu, ab_w_out,
              sc_w_in, sc_conv_w, sc_w_out, final_norm):
    for layer in range(DEPTH):
        x = x + 0.5 * swiglu(rmsnorm(x, ffn1_norm[layer]),
                             ffn1_w_gate[layer], ffn1_w_up[layer], ffn1_w_down[layer])
        h = rmsnorm(x, mix_norm[layer])
        if layer % 2 == 0:
            e = layer // 2
            x = x + parallel_s5_stickbreak(
                h, ab_w_in[e], s5_lambda_re[e], s5_lambda_im[e], s5_log_dt[e],
                s5_b_re[e], s5_b_im[e], s5_c_re[e], s5_c_im[e], s5_d[e], s5_w_glu[e], ab_w_out[e])
        else:
            o = layer // 2
            x = x + short_conv_mixer(h, sc_w_in[o], sc_conv_w[o], sc_w_out[o])
        x = x + 0.5 * swiglu(rmsnorm(x, ffn2_norm[layer]),
                             ffn2_w_gate[layer], ffn2_w_up[layer], ffn2_w_down[layer])
    return rmsnorm(x, final_norm)
```

```python
import functools
import math

import jax
import jax.numpy as jnp
from jax import lax
from jax.experimental import pallas as pl
from jax.experimental.pallas import tpu as pltpu

F32 = jnp.float32
BF16 = jnp.bfloat16

EPS = 1e-6
LANES = 128
D_FF_PAD_MULTIPLE = 128
S5_GROUP = 16
S5_STATE = 64
S5_CHUNK = 16
GROUPS_PER_BLOCK = LANES // S5_GROUP
SB_HEAD_DIM = 64
CONV_K = 3
VMEM_LIMIT_BYTES = 56 * 1024 * 1024


def _params(semantics):
    return pltpu.CompilerParams(dimension_semantics=semantics, vmem_limit_bytes=VMEM_LIMIT_BYTES)


def _resident(shape):
    zeros = (0,) * len(shape)
    return pl.BlockSpec(shape, lambda *_: zeros, pipeline_mode=pl.Buffered(1))


def _rms(x, g):
    r = lax.rsqrt(jnp.mean(x * x, axis=-1, keepdims=True) + EPS)
    return x * r * g


def _dot(a, b):
    return jnp.dot(a, b, preferred_element_type=F32)


def _ffn_body(x_ref, g_ref, wg_ref, wu_ref, wd_ref, *rest, final_norm):
    if final_norm:
        fg_ref, o_ref = rest
    else:
        (o_ref,) = rest
    x = x_ref[...]
    h = _rms(x, g_ref[...]).astype(BF16)
    gate = _dot(h, wg_ref[...])
    up = _dot(h, wu_ref[...])
    act = (gate * jax.nn.sigmoid(gate) * up).astype(BF16)
    y = x + 0.5 * _dot(act, wd_ref[...])
    if final_norm:
        y = _rms(y, fg_ref[...])
    o_ref[...] = y


def _ffn(x, g, wg, wu, wd, final_g=None, *, tm=256):
    t, d = x.shape
    f = wg.shape[1]
    row = pl.BlockSpec((tm, d), lambda i: (i, 0))
    in_specs = [row, _resident((1, d)), _resident((d, f)), _resident((d, f)), _resident((f, d))]
    args = [x, g, wg, wu, wd]
    if final_g is not None:
        in_specs.append(_resident((1, d)))
        args.append(final_g)
    return pl.pallas_call(
        functools.partial(_ffn_body, final_norm=final_g is not None),
        grid=(t // tm,),
        in_specs=in_specs,
        out_specs=row,
        out_shape=jax.ShapeDtypeStruct((t, d), F32),
        compiler_params=_params(("arbitrary",)),
        name="ffn",
    )(*args)


def _proj0_body(x_ref, g_ref, w_ref, u_ref, q_ref, k_ref, v_ref, *, width, scale):
    h = _rms(x_ref[...], g_ref[...]).astype(BF16)
    p = _dot(h, w_ref[...])
    u_ref[...] = p[:, :width]
    q_ref[...] = (p[:, width:2 * width] * scale).astype(BF16)
    k_ref[...] = p[:, 2 * width:3 * width].astype(BF16)
    v_ref[...] = p[:, 3 * width:].astype(BF16)


def _proj0(x, g, w, *, tm=512):
    t, d = x.shape
    width = w.shape[1] // 4
    row = pl.BlockSpec((tm, d), lambda i: (i, 0))
    out = pl.BlockSpec((tm, width), lambda i: (i, 0))
    return pl.pallas_call(
        functools.partial(_proj0_body, width=width, scale=1.0 / math.sqrt(SB_HEAD_DIM)),
        grid=(t // tm,),
        in_specs=[row, _resident((1, d)), _resident(w.shape)],
        out_specs=[out, out, out, out],
        out_shape=[jax.ShapeDtypeStruct((t, width), F32)] + [jax.ShapeDtypeStruct((t, width), BF16)] * 3,
        compiler_params=_params(("arbitrary",)),
        name="proj0",
    )(x, g, w)


def _s5_tables(lam_re, lam_im, log_dt, b_re, b_im, c_re, c_im, d):
    hi = lax.Precision.HIGHEST
    g_all, n_st = lam_re.shape
    p_ch = b_re.shape[-1]
    c = S5_CHUNK
    dt = jnp.exp(log_dt)[:, None]
    mag = jnp.exp(lam_re * dt)
    ab_re = mag * jnp.cos(lam_im * dt)
    ab_im = mag * jnp.sin(lam_im * dt)
    den = lam_re * lam_re + lam_im * lam_im
    nr = ab_re - 1.0
    coef_re = (nr * lam_re + ab_im * lam_im) / den
    coef_im = (ab_im * lam_re - nr * lam_im) / den
    bb_re = coef_re[..., None] * b_re - coef_im[..., None] * b_im
    bb_im = coef_re[..., None] * b_im + coef_im[..., None] * b_re
    pw_re = [jnp.ones_like(ab_re)]
    pw_im = [jnp.zeros_like(ab_im)]
    for _ in range(c):
        pr, pi = pw_re[-1], pw_im[-1]
        pw_re.append(pr * ab_re - pi * ab_im)
        pw_im.append(pr * ab_im + pi * ab_re)
    pw_re = jnp.stack(pw_re)
    pw_im = jnp.stack(pw_im)
    ab_b_re = pw_re[..., None] * bb_re - pw_im[..., None] * bb_im
    ab_b_im = pw_re[..., None] * bb_im + pw_im[..., None] * bb_re
    kern = (jnp.einsum('tgnp,gqn->tgpq', ab_b_re[:c], c_re, precision=hi)
            - jnp.einsum('tgnp,gqn->tgpq', ab_b_im[:c], c_im, precision=hi))
    kern = kern.at[0].add(jax.vmap(jnp.diag)(d.reshape(g_all, p_ch)))
    gpb = GROUPS_PER_BLOCK
    j_blocks = g_all // gpb
    eye = jnp.eye(gpb, dtype=F32)

    def block_diag(a):
        lead = a.shape[:-3]
        r, s = a.shape[-2:]
        a = a.reshape(lead + (j_blocks, gpb, r, s))
        a = a[..., :, :, :, None, :] * eye[:, None, :, None]
        return a.reshape(lead + (j_blocks, gpb * r, gpb * s))

    kblk = block_diag(kern)
    zero = jnp.zeros_like(kblk[0])
    rows = []
    for r in range(c):
        left = kblk[c - 2 - r] if c - 2 - r >= 0 else zero
        rows.append(jnp.concatenate([left, kblk[c - 1 - r]], axis=-1))
    master = jnp.concatenate(rows, axis=-2)

    def state_in(x):
        x = jnp.swapaxes(x[:c][::-1], -1, -2)
        return block_diag(x)
    w_state = jnp.concatenate([state_in(ab_b_re), state_in(ab_b_im)], axis=-1)
    w_state = jnp.moveaxis(w_state, 0, 1).reshape(j_blocks, c * LANES, 2 * gpb * n_st)

    q_re = c_re[None] * pw_re[1:, :, None, :] - c_im[None] * pw_im[1:, :, None, :]
    q_im = c_re[None] * pw_im[1:, :, None, :] + c_im[None] * pw_re[1:, :, None, :]
    ro_re = block_diag(jnp.swapaxes(q_re, -1, -2))
    ro_im = block_diag(jnp.swapaxes(-q_im, -1, -2))
    w_out = jnp.concatenate([ro_re, ro_im], axis=-2)
    w_out = jnp.transpose(w_out, (1, 2, 0, 3)).reshape(j_blocks, 2 * gpb * n_st, c * LANES)
    a_pow = jnp.stack([pw_re[c].reshape(j_blocks, gpb * n_st), pw_im[c].reshape(j_blocks, gpb * n_st)], axis=1)
    return master.astype(BF16), w_state.astype(BF16), w_out.astype(BF16), a_pow


def _s5_body(u_ref, master_ref, ws_ref, wo_ref, ap_ref, y_ref, s_ref, hp_ref, *, chunks):
    c = S5_CHUNK
    half = hp_ref.shape[1] // 2
    u_flat = jnp.concatenate(
        [u_ref[pl.ds(t, chunks, stride=c), :].astype(BF16) for t in range(c)], axis=1)
    s_ref[...] = _dot(u_flat, ws_ref[...])
    a_re = ap_ref[0:1, :]
    a_im = ap_ref[1:2, :]

    def step(k, carry):
        h_re, h_im = carry
        hp_ref[pl.ds(k, 1), :half] = h_re
        hp_ref[pl.ds(k, 1), half:] = h_im
        s = s_ref[pl.ds(k, 1), :]
        return (a_re * h_re - a_im * h_im + s[:, :half], a_re * h_im + a_im * h_re + s[:, half:])

    zero = jnp.zeros((1, half), F32)
    lax.fori_loop(0, chunks, step, (zero, zero), unroll=8)
    h_prev = hp_ref[...].astype(BF16)
    for tp in range(0, c, 2):
        yp = (_dot(u_flat[:, :LANES * (tp + 2)], master_ref[(c - 2 - tp) * LANES:, :])
              + _dot(h_prev, wo_ref[:, tp * LANES:(tp + 2) * LANES]))
        y_ref[pl.ds(tp, chunks, stride=c), :] = yp[:, :LANES]
        y_ref[pl.ds(tp + 1, chunks, stride=c), :] = yp[:, LANES:]


def _s5(u, tables, *, seq):
    t, width = u.shape
    master, w_state, w_out, a_pow = tables
    j_blocks = width // LANES
    batch = t // seq
    chunks = seq // S5_CHUNK
    n_state = w_state.shape[-1]
    io = pl.BlockSpec((seq, LANES), lambda j, b: (b, j))

    def table(a):
        return pl.BlockSpec((None,) + a.shape[1:], lambda j, b: (j, 0, 0))

    return pl.pallas_call(
        functools.partial(_s5_body, chunks=chunks),
        grid=(j_blocks, batch),
        in_specs=[io, table(master), table(w_state), table(w_out), table(a_pow)],
        out_specs=io,
        out_shape=jax.ShapeDtypeStruct((t, width), F32),
        scratch_shapes=[pltpu.VMEM((chunks, n_state), F32), pltpu.VMEM((chunks, n_state), F32)],
        compiler_params=_params(("arbitrary", "arbitrary")),
        name="s5",
    )(u, master, w_state, w_out, a_pow)


def _attn_body(q_ref, k_ref, v_ref, o_ref, *, tile):
    qi = pl.program_id(2)
    q = q_ref[...]
    lane = lax.broadcasted_iota(jnp.int32, q.shape, 1)
    first = lane < SB_HEAD_DIM
    zero_q = jnp.zeros_like(q)
    q_heads = (jnp.where(first, q, zero_q), jnp.where(first, zero_q, q))
    row = lax.broadcasted_iota(jnp.int32, (tile, tile), 0)
    col = lax.broadcasted_iota(jnp.int32, (tile, tile), 1)
    suffix = (row >= col).astype(BF16)
    causal = col < row

    def process(j, state, masked):
        start = pl.multiple_of(j * tile, tile)
        kt = k_ref[pl.ds(start, tile), :]
        vt = v_ref[pl.ds(start, tile), :]
        new = []
        for qm, (carry, acc) in zip(q_heads, state):
            z = lax.dot_general(qm, kt, (((1,), (1,)), ((), ())), preferred_element_type=F32)
            lk = jnp.minimum(-z, 0.0) - jnp.log1p(jnp.exp(-jnp.abs(z)))
            if masked:
                lk = jnp.where(causal, lk, 0.0)
            r = _dot(lk.astype(BF16), suffix) + carry
            w = jnp.exp(z + r)
            if masked:
                w = jnp.where(causal, w, 0.0)
            new.append((r[:, 0:1], acc + _dot(w.astype(BF16), vt)))
        return tuple(new)

    init = tuple((jnp.zeros((tile, 1), F32), jnp.zeros((tile, LANES), F32)) for _ in q_heads)
    state = process(qi, init, True)
    state = lax.fori_loop(0, qi, lambda it, st: process(qi - 1 - it, st, False), state)
    o_ref[...] = jnp.where(first, state[0][1], state[1][1]).astype(o_ref.dtype)


def _attn(q, k, v, *, seq, tile=256):
    t, width = q.shape
    batch = t // seq
    pairs = width // LANES
    q_tiles = seq // tile
    qo = pl.BlockSpec((tile, LANES), lambda b, p, i: (b * q_tiles + i, p))
    kv = pl.BlockSpec((seq, LANES), lambda b, p, i: (b, p))
    return pl.pallas_call(
        functools.partial(_attn_body, tile=tile),
        grid=(batch, pairs, q_tiles),
        in_specs=[qo, kv, kv],
        out_specs=qo,
        out_shape=jax.ShapeDtypeStruct((t, width), BF16),
        compiler_params=_params(("arbitrary", "arbitrary", "arbitrary")),
        name="stickbreak",
    )(q, k, v)


def _mixout_body(x_ref, ya_ref, yb_ref, wglu_ref, woa_ref, wob_ref, o_ref):
    y = ya_ref[...]
    z = 0.5 * y * (1.0 + jnp.tanh(math.sqrt(2.0 / math.pi) * (y + 0.044715 * (y * y * y))))
    gate = _dot(z.astype(BF16), wglu_ref[...])
    ya = (z * jax.nn.sigmoid(gate)).astype(BF16)
    o_ref[...] = x_ref[...] + _dot(ya, woa_ref[...]) + _dot(yb_ref[...], wob_ref[...])


def _mixout(x, ya, yb, wglu, woa, wob, *, tm=512):
    t, d = x.shape
    width = ya.shape[1]
    row = pl.BlockSpec((tm, d), lambda i: (i, 0))
    half = pl.BlockSpec((tm, width), lambda i: (i, 0))
    return pl.pallas_call(
        _mixout_body,
        grid=(t // tm,),
        in_specs=[row, half, half, _resident(wglu.shape), _resident(woa.shape), _resident(wob.shape)],
        out_specs=row,
        out_shape=jax.ShapeDtypeStruct((t, d), F32),
        compiler_params=_params(("arbitrary",)),
        name="mixout0",
    )(x, ya, yb, wglu, woa, wob)


def _conv_body(x_ref, g_ref, win_ref, cw_ref, wout_ref, o_ref, prev_ref, *, tiles_per_seq, ch):
    i = pl.program_id(0)

    @pl.when(i % tiles_per_seq == 0)
    def _():
        prev_ref[...] = jnp.zeros_like(prev_ref)

    x = x_ref[...]
    tm = x.shape[0]
    h = _rms(x, g_ref[...]).astype(BF16)
    p = _dot(h, win_ref[...])
    b_gate = p[:, :ch]
    cv = p[:, ch:2 * ch] * p[:, 2 * ch:]
    prev = prev_ref[...]
    row = lax.broadcasted_iota(jnp.int32, (tm, 1), 0)
    back1 = jnp.where(row == 0, prev[7:8, :], pltpu.roll(cv, 1, 0))
    back2 = jnp.where(row == 0, prev[6:7, :], jnp.where(row == 1, prev[7:8, :], pltpu.roll(cv, 2, 0)))
    y = cw_ref[0:1, :] * back2 + cw_ref[1:2, :] * back1 + cw_ref[2:3, :] * cv
    prev_ref[...] = cv[tm - 8:, :]
    o_ref[...] = x + _dot((b_gate * y).astype(BF16), wout_ref[...])


def _conv_mixer(x, g, win, cw, wout, *, seq, tm=512):
    t, d = x.shape
    ch = wout.shape[0]
    row = pl.BlockSpec((tm, d), lambda i: (i, 0))
    return pl.pallas_call(
        functools.partial(_conv_body, tiles_per_seq=seq // tm, ch=ch),
        grid=(t // tm,),
        in_specs=[row, _resident((1, d)), _resident(win.shape), _resident(cw.shape), _resident(wout.shape)],
        out_specs=row,
        out_shape=jax.ShapeDtypeStruct((t, d), F32),
        scratch_shapes=[pltpu.VMEM((8, ch), F32)],
        compiler_params=_params(("arbitrary",)),
        name="conv_mixer",
    )(x, g, win, cw, wout)


def _pad_ffn(wg, wu, wd):
    f = wg.shape[1]
    fp = -(-f // D_FF_PAD_MULTIPLE) * D_FF_PAD_MULTIPLE
    pad = fp - f
    return (jnp.pad(wg.astype(BF16), ((0, 0), (0, pad))), jnp.pad(wu.astype(BF16), ((0, 0), (0, pad))),
            jnp.pad(wd.astype(BF16), ((0, pad), (0, 0))))


def kernel(x, ffn1_norm, ffn1_w_gate, ffn1_w_up, ffn1_w_down, mix_norm, ffn2_norm, ffn2_w_gate, ffn2_w_up,
           ffn2_w_down, ab_w_in, s5_lambda_re, s5_lambda_im, s5_log_dt, s5_b_re, s5_b_im, s5_c_re, s5_c_im, s5_d,
           s5_w_glu, ab_w_out, sc_w_in, sc_conv_w, sc_w_out, final_norm):
    batch, seq, d = x.shape
    depth = ffn1_norm.shape[0]
    xt = x.reshape(batch * seq, d)
    for layer in range(depth):
        xt = _ffn(xt, ffn1_norm[layer][None], *_pad_ffn(ffn1_w_gate[layer], ffn1_w_up[layer], ffn1_w_down[layer]))
        g_mix = mix_norm[layer][None]
        if layer % 2 == 0:
            e = layer // 2
            u, q, k, v = _proj0(xt, g_mix, ab_w_in[e].astype(BF16))
            tables = _s5_tables(s5_lambda_re[e], s5_lambda_im[e], s5_log_dt[e], s5_b_re[e], s5_b_im[e],
                                s5_c_re[e], s5_c_im[e], s5_d[e])
            ya = _s5(u, tables, seq=seq)
            yb = _attn(q, k, v, seq=seq)
            width = u.shape[1]
            w_out = ab_w_out[e].astype(BF16)
            xt = _mixout(xt, ya, yb, s5_w_glu[e].astype(BF16), w_out[:width], w_out[width:])
        else:
            o = layer // 2
            xt = _conv_mixer(xt, g_mix, sc_w_in[o].astype(BF16), sc_conv_w[o], sc_w_out[o].astype(BF16), seq=seq)
        last = layer == depth - 1
        xt = _ffn(xt, ffn2_norm[layer][None], *_pad_ffn(ffn2_w_gate[layer], ffn2_w_up[layer], ffn2_w_down[layer]),
                  final_g=final_norm[None] if last else None)
    return xt.reshape(batch, seq, d)
```

```python
import functools
import math

import jax
import jax.numpy as jnp
from jax import lax
from jax.experimental import pallas as pl
from jax.experimental.pallas import tpu as pltpu

F32 = jnp.float32
BF16 = jnp.bfloat16

EPS = 1e-6
LANES = 128
D_FF_PAD_MULTIPLE = 128
S5_GROUP = 16
S5_STATE = 64
S5_CHUNK = 16
GROUPS_PER_BLOCK = LANES // S5_GROUP
SB_HEAD_DIM = 64
MASKED_LOG_WEIGHT = -1e30
CONV_K = 3
VMEM_LIMIT_BYTES = 56 * 1024 * 1024


def _params(semantics):
    return pltpu.CompilerParams(dimension_semantics=semantics, vmem_limit_bytes=VMEM_LIMIT_BYTES)


def _resident(shape):
    zeros = (0,) * len(shape)
    return pl.BlockSpec(shape, lambda *_: zeros, pipeline_mode=pl.Buffered(1))


def _rms(x, g):
    r = lax.rsqrt(jnp.mean(x * x, axis=-1, keepdims=True) + EPS)
    return x * r * g


def _dot(a, b):
    return jnp.dot(a, b, preferred_element_type=F32)


def _ffn_body(x_ref, g_ref, wg_ref, wu_ref, wd_ref, *rest, final_norm):
    if final_norm:
        fg_ref, o_ref = rest
    else:
        (o_ref,) = rest
    x = x_ref[...]
    h = _rms(x, g_ref[...]).astype(BF16)
    gate = _dot(h, wg_ref[...])
    up = _dot(h, wu_ref[...])
    act = (gate * jax.nn.sigmoid(gate) * up).astype(BF16)
    y = x + 0.5 * _dot(act, wd_ref[...])
    if final_norm:
        y = _rms(y, fg_ref[...])
    o_ref[...] = y


def _ffn(x, g, wg, wu, wd, final_g=None, *, tm=512):
    t, d = x.shape
    f = wg.shape[1]
    row = pl.BlockSpec((tm, d), lambda i: (i, 0))
    in_specs = [row, _resident((1, d)), _resident((d, f)), _resident((d, f)), _resident((f, d))]
    args = [x, g, wg, wu, wd]
    if final_g is not None:
        in_specs.append(_resident((1, d)))
        args.append(final_g)
    return pl.pallas_call(
        functools.partial(_ffn_body, final_norm=final_g is not None),
        grid=(t // tm,),
        in_specs=in_specs,
        out_specs=row,
        out_shape=jax.ShapeDtypeStruct((t, d), F32),
        compiler_params=_params(("arbitrary",)),
        name="ffn",
    )(*args)


def _proj0_body(x_ref, g_ref, w_ref, u_ref, q_ref, k_ref, v_ref, *, width, scale):
    h = _rms(x_ref[...], g_ref[...]).astype(BF16)
    p = _dot(h, w_ref[...])
    u_ref[...] = p[:, :width]
    q_ref[...] = (p[:, width:2 * width] * scale).astype(BF16)
    k_ref[...] = p[:, 2 * width:3 * width].astype(BF16)
    v_ref[...] = p[:, 3 * width:].astype(BF16)


def _proj0(x, g, w, *, tm=512):
    t, d = x.shape
    width = w.shape[1] // 4
    row = pl.BlockSpec((tm, d), lambda i: (i, 0))
    out = pl.BlockSpec((tm, width), lambda i: (i, 0))
    return pl.pallas_call(
        functools.partial(_proj0_body, width=width, scale=1.0 / math.sqrt(SB_HEAD_DIM)),
        grid=(t // tm,),
        in_specs=[row, _resident((1, d)), _resident(w.shape)],
        out_specs=[out, out, out, out],
        out_shape=[jax.ShapeDtypeStruct((t, width), F32)] + [jax.ShapeDtypeStruct((t, width), BF16)] * 3,
        compiler_params=_params(("arbitrary",)),
        name="proj0",
    )(x, g, w)


def _s5_tables(lam_re, lam_im, log_dt, b_re, b_im, c_re, c_im, d):
    hi = lax.Precision.HIGHEST
    g_all, n_st = lam_re.shape
    p_ch = b_re.shape[-1]
    c = S5_CHUNK
    dt = jnp.exp(log_dt)[:, None]
    mag = jnp.exp(lam_re * dt)
    ab_re = mag * jnp.cos(lam_im * dt)
    ab_im = mag * jnp.sin(lam_im * dt)
    den = lam_re * lam_re + lam_im * lam_im
    nr = ab_re - 1.0
    coef_re = (nr * lam_re + ab_im * lam_im) / den
    coef_im = (ab_im * lam_re - nr * lam_im) / den
    bb_re = coef_re[..., None] * b_re - coef_im[..., None] * b_im
    bb_im = coef_re[..., None] * b_im + coef_im[..., None] * b_re
    pw_re = [jnp.ones_like(ab_re)]
    pw_im = [jnp.zeros_like(ab_im)]
    for _ in range(c):
        pr, pi = pw_re[-1], pw_im[-1]
        pw_re.append(pr * ab_re - pi * ab_im)
        pw_im.append(pr * ab_im + pi * ab_re)
    pw_re = jnp.stack(pw_re)
    pw_im = jnp.stack(pw_im)
    ab_b_re = pw_re[..., None] * bb_re - pw_im[..., None] * bb_im
    ab_b_im = pw_re[..., None] * bb_im + pw_im[..., None] * bb_re
    kern = (jnp.einsum('tgnp,gqn->tgpq', ab_b_re[:c], c_re, precision=hi)
            - jnp.einsum('tgnp,gqn->tgpq', ab_b_im[:c], c_im, precision=hi))
    kern = kern.at[0].add(jax.vmap(jnp.diag)(d.reshape(g_all, p_ch)))
    gpb = GROUPS_PER_BLOCK
    j_blocks = g_all // gpb
    eye = jnp.eye(gpb, dtype=F32)

    kern_pad = jnp.concatenate([jnp.zeros_like(kern[:1]), kern], axis=0)
    lag = (c - 1) + jnp.arange(2)[None, :] - jnp.arange(c)[:, None]
    sel = kern_pad[lag].reshape(c, 2, j_blocks, gpb, p_ch, p_ch)
    sel = jnp.transpose(sel, (2, 0, 3, 4, 1, 5))
    master = (sel[:, :, :, :, :, None, :] * eye[:, None, None, :, None]).astype(BF16)
    master = master.reshape(j_blocks, c * LANES, 2 * LANES)

    st = jnp.stack([ab_b_re[:c][::-1], ab_b_im[:c][::-1]])
    st = st.reshape(2, c, j_blocks, gpb, n_st, p_ch)
    st = jnp.transpose(st, (2, 1, 3, 5, 0, 4))
    w_state = (st[:, :, :, :, :, None, :] * eye[:, None, None, :, None]).astype(BF16)
    w_state = w_state.reshape(j_blocks, c * LANES, 2 * gpb * n_st)

    q_re = c_re[None] * pw_re[1:, :, None, :] - c_im[None] * pw_im[1:, :, None, :]
    q_im = c_re[None] * pw_im[1:, :, None, :] + c_im[None] * pw_re[1:, :, None, :]
    ro = jnp.stack([q_re, -q_im]).reshape(2, c, j_blocks, gpb, p_ch, n_st)
    ro = jnp.transpose(ro, (2, 0, 3, 5, 1, 4))
    w_out = (ro[:, :, :, :, :, None, :] * eye[:, None, None, :, None]).astype(BF16)
    w_out = w_out.reshape(j_blocks, 2 * gpb * n_st, c * LANES)
    a_pow = jnp.stack([pw_re[c].reshape(j_blocks, gpb * n_st), pw_im[c].reshape(j_blocks, gpb * n_st)], axis=1)
    return master, w_state, w_out, a_pow


def _s5_body(u_ref, master_ref, ws_ref, wo_ref, ap_ref, y_ref, s_ref, hp_ref, *, chunks):
    c = S5_CHUNK
    half = hp_ref.shape[1] // 2
    u_flat = jnp.concatenate(
        [u_ref[pl.ds(t, chunks, stride=c), :].astype(BF16) for t in range(c)], axis=1)
    s_ref[...] = _dot(u_flat, ws_ref[...])
    a_re = ap_ref[0:1, :]
    a_im = ap_ref[1:2, :]

    def step(k, carry):
        h_re, h_im = carry
        hp_ref[pl.ds(k, 1), :half] = h_re
        hp_ref[pl.ds(k, 1), half:] = h_im
        s = s_ref[pl.ds(k, 1), :]
        return (a_re * h_re - a_im * h_im + s[:, :half], a_re * h_im + a_im * h_re + s[:, half:])

    zero = jnp.zeros((1, half), F32)
    lax.fori_loop(0, chunks, step, (zero, zero), unroll=8)
    h_prev = hp_ref[...].astype(BF16)
    for tp in range(0, c, 2):
        yp = (_dot(u_flat[:, :LANES * (tp + 2)], master_ref[(c - 2 - tp) * LANES:, :])
              + _dot(h_prev, wo_ref[:, tp * LANES:(tp + 2) * LANES]))
        y_ref[pl.ds(tp, chunks, stride=c), :] = yp[:, :LANES]
        y_ref[pl.ds(tp + 1, chunks, stride=c), :] = yp[:, LANES:]


def _s5(u, tables, *, seq):
    t, width = u.shape
    master, w_state, w_out, a_pow = tables
    j_blocks = width // LANES
    batch = t // seq
    chunks = seq // S5_CHUNK
    n_state = w_state.shape[-1]
    io = pl.BlockSpec((seq, LANES), lambda j, b: (b, j))

    def table(a):
        return pl.BlockSpec((None,) + a.shape[1:], lambda j, b: (j, 0, 0))

    return pl.pallas_call(
        functools.partial(_s5_body, chunks=chunks),
        grid=(j_blocks, batch),
        in_specs=[io, table(master), table(w_state), table(w_out), table(a_pow)],
        out_specs=io,
        out_shape=jax.ShapeDtypeStruct((t, width), F32),
        scratch_shapes=[pltpu.VMEM((chunks, n_state), F32), pltpu.VMEM((chunks, n_state), F32)],
        compiler_params=_params(("arbitrary", "arbitrary")),
        name="s5",
    )(u, master, w_state, w_out, a_pow)


def _attn_body(q_ref, k_ref, v_ref, o_ref, acc_ref, sp_ref, lb_ref, w_ref, *, tile, blocks):
    qi = pl.program_id(2)
    lane = lax.broadcasted_iota(jnp.int32, (tile, LANES), 1)
    first = lane < SB_HEAD_DIM
    heads = []
    for p in range(blocks):
        q = q_ref[:, p * LANES:(p + 1) * LANES]
        zero_q = jnp.zeros_like(q)
        heads += [(p, jnp.where(first, q, zero_q)), (p, jnp.where(first, zero_q, q))]
    row = lax.broadcasted_iota(jnp.int32, (tile, tile), 0)
    col = lax.broadcasted_iota(jnp.int32, (tile, tile), 1)
    neg_later = jnp.where(row > col, -1.0, 0.0).astype(BF16)
    causal = col < row

    def key_rows(j):
        return pl.ds(pl.multiple_of(j * tile, tile), tile)

    def scores(j, masked):
        for h, (p, qm) in enumerate(heads):
            kt = k_ref[key_rows(j), p * LANES:(p + 1) * LANES]
            z = lax.dot_general(qm, kt, (((1,), (1,)), ((), ())), preferred_element_type=F32)
            sp = jnp.maximum(z, 0.0) + jnp.log(1.0 + jnp.exp(-jnp.abs(z)))
            log_beta = z - sp
            if masked:
                sp = jnp.where(causal, sp, 0.0)
                log_beta = jnp.where(causal, log_beta, MASKED_LOG_WEIGHT)
            sp_ref[h] = sp.astype(BF16)
            lb_ref[h] = log_beta

    def weights(carries):
        new = []
        for h in range(len(heads)):
            sp = sp_ref[h]
            r = _dot(sp, neg_later) + carries[h]
            w_ref[h] = jnp.exp(lb_ref[h] + r).astype(BF16)
            new.append(r[:, 0:1] - sp[:, 0:1].astype(F32))
        return tuple(new)

    def accumulate(j):
        for h, (p, _) in enumerate(heads):
            acc_ref[h] += _dot(w_ref[h], v_ref[key_rows(j), p * LANES:(p + 1) * LANES])

    acc_ref[...] = jnp.zeros_like(acc_ref)
    w_ref[...] = jnp.zeros_like(w_ref)
    scores(qi, True)

    def trip(it, carries):
        j = qi - it
        accumulate(jnp.minimum(j + 1, qi))
        carries = weights(carries)
        scores(j - 1, False)
        return carries

    carries = lax.fori_loop(0, qi, trip, tuple(jnp.zeros((tile, 1), F32) for _ in heads))
    accumulate(jnp.minimum(1, qi))
    weights(carries)
    accumulate(0)
    for p in range(blocks):
        o_ref[:, p * LANES:(p + 1) * LANES] = jnp.where(first, acc_ref[2 * p], acc_ref[2 * p + 1]).astype(o_ref.dtype)


def _attn(q, k, v, *, seq, tile=256, blocks=4):
    t, width = q.shape
    batch = t // seq
    groups = width // (blocks * LANES)
    q_tiles = seq // tile
    qo = pl.BlockSpec((tile, blocks * LANES), lambda b, p, i: (b * q_tiles + i, p))
    kv = pl.BlockSpec((seq, blocks * LANES), lambda b, p, i: (b, p))
    return pl.pallas_call(
        functools.partial(_attn_body, tile=tile, blocks=blocks),
        grid=(batch, groups, q_tiles),
        in_specs=[qo, kv, kv],
        out_specs=qo,
        out_shape=jax.ShapeDtypeStruct((t, width), BF16),
        scratch_shapes=[pltpu.VMEM((2 * blocks, tile, LANES), F32),
                        pltpu.VMEM((2 * blocks, tile, tile), BF16),
                        pltpu.VMEM((2 * blocks, tile, tile), F32),
                        pltpu.VMEM((2 * blocks, tile, tile), BF16)],
        compiler_params=_params(("arbitrary", "arbitrary", "arbitrary")),
        name="stickbreak",
    )(q, k, v)


def _mixout_body(x_ref, ya_ref, yb_ref, wglu_ref, woa_ref, wob_ref, o_ref):
    y = ya_ref[...]
    z = 0.5 * y * (1.0 + jnp.tanh(math.sqrt(2.0 / math.pi) * (y + 0.044715 * (y * y * y))))
    gate = _dot(z.astype(BF16), wglu_ref[...])
    ya = (z * jax.nn.sigmoid(gate)).astype(BF16)
    o_ref[...] = x_ref[...] + _dot(ya, woa_ref[...]) + _dot(yb_ref[...], wob_ref[...])


def _mixout(x, ya, yb, wglu, woa, wob, *, tm=512):
    t, d = x.shape
    width = ya.shape[1]
    row = pl.BlockSpec((tm, d), lambda i: (i, 0))
    half = pl.BlockSpec((tm, width), lambda i: (i, 0))
    return pl.pallas_call(
        _mixout_body,
        grid=(t // tm,),
        in_specs=[row, half, half, _resident(wglu.shape), _resident(woa.shape), _resident(wob.shape)],
        out_specs=row,
        out_shape=jax.ShapeDtypeStruct((t, d), F32),
        compiler_params=_params(("arbitrary",)),
        name="mixout0",
    )(x, ya, yb, wglu, woa, wob)


def _conv_body(x_ref, g_ref, win_ref, cw_ref, wout_ref, o_ref, prev_ref, *, tiles_per_seq, ch):
    i = pl.program_id(0)

    @pl.when(i % tiles_per_seq == 0)
    def _():
        prev_ref[...] = jnp.zeros_like(prev_ref)

    x = x_ref[...]
    tm = x.shape[0]
    h = _rms(x, g_ref[...]).astype(BF16)
    p = _dot(h, win_ref[...])
    b_gate = p[:, :ch]
    cv = p[:, ch:2 * ch] * p[:, 2 * ch:]
    prev = prev_ref[...]
    row = lax.broadcasted_iota(jnp.int32, (tm, 1), 0)
    back1 = jnp.where(row == 0, prev[7:8, :], pltpu.roll(cv, 1, 0))
    back2 = jnp.where(row == 0, prev[6:7, :], jnp.where(row == 1, prev[7:8, :], pltpu.roll(cv, 2, 0)))
    y = cw_ref[0:1, :] * back2 + cw_ref[1:2, :] * back1 + cw_ref[2:3, :] * cv
    prev_ref[...] = cv[tm - 8:, :]
    o_ref[...] = x + _dot((b_gate * y).astype(BF16), wout_ref[...])


def _conv_mixer(x, g, win, cw, wout, *, seq, tm=512):
    t, d = x.shape
    ch = wout.shape[0]
    row = pl.BlockSpec((tm, d), lambda i: (i, 0))
    return pl.pallas_call(
        functools.partial(_conv_body, tiles_per_seq=seq // tm, ch=ch),
        grid=(t // tm,),
        in_specs=[row, _resident((1, d)), _resident(win.shape), _resident(cw.shape), _resident(wout.shape)],
        out_specs=row,
        out_shape=jax.ShapeDtypeStruct((t, d), F32),
        scratch_shapes=[pltpu.VMEM((8, ch), F32)],
        compiler_params=_params(("arbitrary",)),
        name="conv_mixer",
    )(x, g, win, cw, wout)


def _pad_ffn(wg, wu, wd):
    f = wg.shape[1]
    fp = -(-f // D_FF_PAD_MULTIPLE) * D_FF_PAD_MULTIPLE
    pad = fp - f
    d = wg.shape[0]
    cols = jnp.zeros((d, pad), BF16)
    return (jnp.concatenate([wg.astype(BF16), cols], axis=1), jnp.concatenate([wu.astype(BF16), cols], axis=1),
            jnp.concatenate([wd.astype(BF16), cols.T], axis=0))


def kernel(x, ffn1_norm, ffn1_w_gate, ffn1_w_up, ffn1_w_down, mix_norm, ffn2_norm, ffn2_w_gate, ffn2_w_up,
           ffn2_w_down, ab_w_in, s5_lambda_re, s5_lambda_im, s5_log_dt, s5_b_re, s5_b_im, s5_c_re, s5_c_im, s5_d,
           s5_w_glu, ab_w_out, sc_w_in, sc_conv_w, sc_w_out, final_norm):
    batch, seq, d = x.shape
    depth = ffn1_norm.shape[0]
    xt = x.reshape(batch * seq, d)
    for layer in range(depth):
        xt = _ffn(xt, ffn1_norm[layer][None], *_pad_ffn(ffn1_w_gate[layer], ffn1_w_up[layer], ffn1_w_down[layer]))
        g_mix = mix_norm[layer][None]
        if layer % 2 == 0:
            e = layer // 2
            u, q, k, v = _proj0(xt, g_mix, ab_w_in[e].astype(BF16))
            tables = _s5_tables(s5_lambda_re[e], s5_lambda_im[e], s5_log_dt[e], s5_b_re[e], s5_b_im[e],
                                s5_c_re[e], s5_c_im[e], s5_d[e])
            ya = _s5(u, tables, seq=seq)
            yb = _attn(q, k, v, seq=seq)
            width = u.shape[1]
            w_out = ab_w_out[e].astype(BF16)
            xt = _mixout(xt, ya, yb, s5_w_glu[e].astype(BF16), w_out[:width], w_out[width:])
        else:
            o = layer // 2
            xt = _conv_mixer(xt, g_mix, sc_w_in[o].astype(BF16), sc_conv_w[o], sc_w_out[o].astype(BF16), seq=seq)
        last = layer == depth - 1
        xt = _ffn(xt, ffn2_norm[layer][None], *_pad_ffn(ffn2_w_gate[layer], ffn2_w_up[layer], ffn2_w_down[layer]),
                  final_g=final_norm[None] if last else None)
    return xt.reshape(batch, seq, d)
```

```python
import functools
import math

import jax
import jax.numpy as jnp
from jax import lax
from jax.experimental import pallas as pl
from jax.experimental.pallas import tpu as pltpu

F32 = jnp.float32
BF16 = jnp.bfloat16

EPS = 1e-6
LANES = 128
D_FF_PAD_MULTIPLE = 128
S5_GROUP = 16
S5_STATE = 64
S5_CHUNK = 16
GROUPS_PER_BLOCK = LANES // S5_GROUP
SB_HEAD_DIM = 64
MASKED_LOG_WEIGHT = -1e30
CONV_K = 3
VMEM_LIMIT_BYTES = 56 * 1024 * 1024


def _params(semantics):
    return pltpu.CompilerParams(dimension_semantics=semantics, vmem_limit_bytes=VMEM_LIMIT_BYTES)


def _resident(shape):
    zeros = (0,) * len(shape)
    return pl.BlockSpec(shape, lambda *_: zeros, pipeline_mode=pl.Buffered(1))


def _rms(x, g):
    r = lax.rsqrt(jnp.mean(x * x, axis=-1, keepdims=True) + EPS)
    return x * r * g


def _dot(a, b):
    return jnp.dot(a, b, preferred_element_type=F32)


def _ffn_body(x_ref, g_ref, wg_ref, wu_ref, wd_ref, *rest, final_norm):
    if final_norm:
        fg_ref, o_ref = rest
    else:
        (o_ref,) = rest
    x = x_ref[...]
    h = _rms(x, g_ref[...]).astype(BF16)
    gate = _dot(h, wg_ref[...])
    up = _dot(h, wu_ref[...])
    act = (gate * jax.nn.sigmoid(gate) * up).astype(BF16)
    y = x + 0.5 * _dot(act, wd_ref[...])
    if final_norm:
        y = _rms(y, fg_ref[...])
    o_ref[...] = y


def _ffn(x, g, weights, layer, final_g=None, *, tm=512):
    t, d = x.shape
    wg, wu, wd = weights
    f = wg.shape[2]
    row = pl.BlockSpec((tm, d), lambda i: (i, 0))

    def of_layer(shape):
        return pl.BlockSpec((None,) + shape, lambda i: (layer, 0, 0), pipeline_mode=pl.Buffered(1))

    in_specs = [row, _resident((1, d)), of_layer((d, f)), of_layer((d, f)), of_layer((f, d))]
    args = [x, g, wg, wu, wd]
    if final_g is not None:
        in_specs.append(_resident((1, d)))
        args.append(final_g)
    return pl.pallas_call(
        functools.partial(_ffn_body, final_norm=final_g is not None),
        grid=(t // tm,),
        in_specs=in_specs,
        out_specs=row,
        out_shape=jax.ShapeDtypeStruct((t, d), F32),
        compiler_params=_params(("arbitrary",)),
        name="ffn",
    )(*args)


def _proj0_body(x_ref, g_ref, w_ref, u_ref, q_ref, k_ref, v_ref, *, width, scale):
    h = _rms(x_ref[...], g_ref[...]).astype(BF16)
    p = _dot(h, w_ref[...])
    u_ref[...] = p[:, :width]
    q_ref[...] = (p[:, width:2 * width] * scale).astype(BF16)
    k_ref[...] = p[:, 2 * width:3 * width].astype(BF16)
    v_ref[...] = p[:, 3 * width:].astype(BF16)


def _proj0(x, g, w, *, tm=512):
    t, d = x.shape
    width = w.shape[1] // 4
    row = pl.BlockSpec((tm, d), lambda i: (i, 0))
    out = pl.BlockSpec((tm, width), lambda i: (i, 0))
    return pl.pallas_call(
        functools.partial(_proj0_body, width=width, scale=1.0 / math.sqrt(SB_HEAD_DIM)),
        grid=(t // tm,),
        in_specs=[row, _resident((1, d)), _resident(w.shape)],
        out_specs=[out, out, out, out],
        out_shape=[jax.ShapeDtypeStruct((t, width), F32)] + [jax.ShapeDtypeStruct((t, width), BF16)] * 3,
        compiler_params=_params(("arbitrary",)),
        name="proj0",
    )(x, g, w)


def _s5_tables(lam_re, lam_im, log_dt, b_re, b_im, c_re, c_im, d):
    hi = lax.Precision.HIGHEST
    g_all, n_st = lam_re.shape
    p_ch = b_re.shape[-1]
    c = S5_CHUNK
    dt = jnp.exp(log_dt)[:, None]
    mag = jnp.exp(lam_re * dt)
    ab_re = mag * jnp.cos(lam_im * dt)
    ab_im = mag * jnp.sin(lam_im * dt)
    den = lam_re * lam_re + lam_im * lam_im
    nr = ab_re - 1.0
    coef_re = (nr * lam_re + ab_im * lam_im) / den
    coef_im = (ab_im * lam_re - nr * lam_im) / den
    bb_re = coef_re[..., None] * b_re - coef_im[..., None] * b_im
    bb_im = coef_re[..., None] * b_im + coef_im[..., None] * b_re
    pw_re = [jnp.ones_like(ab_re)]
    pw_im = [jnp.zeros_like(ab_im)]
    for _ in range(c):
        pr, pi = pw_re[-1], pw_im[-1]
        pw_re.append(pr * ab_re - pi * ab_im)
        pw_im.append(pr * ab_im + pi * ab_re)
    pw_re = jnp.stack(pw_re)
    pw_im = jnp.stack(pw_im)
    ab_b_re = pw_re[..., None] * bb_re - pw_im[..., None] * bb_im
    ab_b_im = pw_re[..., None] * bb_im + pw_im[..., None] * bb_re
    kern = (jnp.einsum('tgnp,gqn->tgpq', ab_b_re[:c], c_re, precision=hi)
            - jnp.einsum('tgnp,gqn->tgpq', ab_b_im[:c], c_im, precision=hi))
    kern = kern.at[0].add(jax.vmap(jnp.diag)(d.reshape(g_all, p_ch)))
    gpb = GROUPS_PER_BLOCK
    j_blocks = g_all // gpb

    kern_pad = jnp.concatenate([jnp.zeros_like(kern[:1]), kern], axis=0)
    lag = (c - 1) + jnp.arange(2)[None, :] - jnp.arange(c)[:, None]
    sel = kern_pad[lag].reshape(c, 2, j_blocks, gpb, p_ch, p_ch)
    master_c = jnp.transpose(sel, (2, 0, 3, 4, 1, 5)).reshape(j_blocks, c * LANES, 2 * p_ch)
    st = jnp.stack([ab_b_re[:c][::-1], ab_b_im[:c][::-1]])
    st = st.reshape(2, c, j_blocks, gpb, n_st, p_ch)
    state_c = jnp.transpose(st, (2, 1, 3, 5, 0, 4)).reshape(j_blocks, c * LANES, 2 * n_st)
    q_re = c_re[None] * pw_re[1:, :, None, :] - c_im[None] * pw_im[1:, :, None, :]
    q_im = c_re[None] * pw_im[1:, :, None, :] + c_im[None] * pw_re[1:, :, None, :]
    ro = jnp.stack([q_re, -q_im]).reshape(2, c, j_blocks, gpb, p_ch, n_st)
    out_c = jnp.transpose(ro, (2, 0, 3, 5, 1, 4)).reshape(j_blocks, 2 * gpb * n_st, c * p_ch)
    a_pow = jnp.stack([pw_re[c].reshape(j_blocks, gpb * n_st), pw_im[c].reshape(j_blocks, gpb * n_st)], axis=1)
    return master_c.astype(BF16), state_c.astype(BF16), out_c.astype(BF16), a_pow


def _log2(n):
    assert n > 0 and n & (n - 1) == 0, n
    return n.bit_length() - 1


def _expand_block_diag(dst_ref, src_ref, *, inner, row_div, col_div, row_chunk=256):
    rows, k = src_ref.shape
    cols = dst_ref.shape[1]
    block = cols // (k // inner)
    kk = lax.broadcasted_iota(jnp.int32, (k, cols), 0)
    kc = lax.broadcasted_iota(jnp.int32, (k, cols), 1)
    spread = (((kk >> _log2(inner)) == (kc >> _log2(block)))
              & ((kk & (inner - 1)) == (kc & (inner - 1)))).astype(BF16)
    row = lax.broadcasted_iota(jnp.int32, (row_chunk, cols), 0)
    col = lax.broadcasted_iota(jnp.int32, (row_chunk, cols), 1)
    col_group = (col >> _log2(col_div)) & (GROUPS_PER_BLOCK - 1)
    for r0 in range(0, rows, row_chunk):
        keep = (((row + r0) >> _log2(row_div)) & (GROUPS_PER_BLOCK - 1)) == col_group
        tiled = _dot(src_ref[r0:r0 + row_chunk, :], spread)
        dst_ref[r0:r0 + row_chunk, :] = jnp.where(keep, tiled, 0.0).astype(dst_ref.dtype)


def _s5_body(u_ref, mc_ref, sc_ref, oc_ref, ap_ref, y_ref, master_ref, ws_ref, wo_ref, s_ref, hp_ref, *, chunks):
    c = S5_CHUNK
    half = hp_ref.shape[1] // 2

    @pl.when(pl.program_id(1) == 0)
    def _():
        _expand_block_diag(master_ref, mc_ref, inner=S5_GROUP, row_div=S5_GROUP, col_div=S5_GROUP)
        _expand_block_diag(ws_ref, sc_ref, inner=S5_STATE, row_div=S5_GROUP, col_div=S5_STATE)
        _expand_block_diag(wo_ref, oc_ref, inner=S5_GROUP, row_div=S5_STATE, col_div=S5_GROUP)

    u_flat = jnp.concatenate(
        [u_ref[pl.ds(t, chunks, stride=c), :].astype(BF16) for t in range(c)], axis=1)
    s_ref[...] = _dot(u_flat, ws_ref[...])
    a_re = ap_ref[0:1, :]
    a_im = ap_ref[1:2, :]

    def step(k, carry):
        h_re, h_im = carry
        hp_ref[pl.ds(k, 1), :half] = h_re
        hp_ref[pl.ds(k, 1), half:] = h_im
        s = s_ref[pl.ds(k, 1), :]
        return (a_re * h_re - a_im * h_im + s[:, :half], a_re * h_im + a_im * h_re + s[:, half:])

    zero = jnp.zeros((1, half), F32)
    lax.fori_loop(0, chunks, step, (zero, zero), unroll=8)
    h_prev = hp_ref[...].astype(BF16)
    for tp in range(0, c, 2):
        yp = (_dot(u_flat[:, :LANES * (tp + 2)], master_ref[(c - 2 - tp) * LANES:, :])
              + _dot(h_prev, wo_ref[:, tp * LANES:(tp + 2) * LANES]))
        y_ref[pl.ds(tp, chunks, stride=c), :] = yp[:, :LANES]
        y_ref[pl.ds(tp + 1, chunks, stride=c), :] = yp[:, LANES:]


def _s5(u, tables, *, seq):
    t, width = u.shape
    master_c, state_c, out_c, a_pow = tables
    j_blocks = width // LANES
    batch = t // seq
    chunks = seq // S5_CHUNK
    n_state = out_c.shape[1]
    rows = S5_CHUNK * LANES
    io = pl.BlockSpec((seq, LANES), lambda j, b: (b, j))

    def table(a):
        return pl.BlockSpec((None,) + a.shape[1:], lambda j, b: (j, 0, 0))

    return pl.pallas_call(
        functools.partial(_s5_body, chunks=chunks),
        grid=(j_blocks, batch),
        in_specs=[io, table(master_c), table(state_c), table(out_c), table(a_pow)],
        out_specs=io,
        out_shape=jax.ShapeDtypeStruct((t, width), F32),
        scratch_shapes=[pltpu.VMEM((rows, 2 * LANES), BF16),
                        pltpu.VMEM((rows, n_state), BF16),
                        pltpu.VMEM((n_state, rows), BF16),
                        pltpu.VMEM((chunks, n_state), F32),
                        pltpu.VMEM((chunks, n_state), F32)],
        compiler_params=_params(("arbitrary", "arbitrary")),
        name="s5",
    )(u, master_c, state_c, out_c, a_pow)


def _attn_body(q_ref, k_ref, v_ref, o_ref, acc_ref, sp_ref, lb_ref, w_ref, *, tile, blocks):
    qi = pl.program_id(2)
    lane = lax.broadcasted_iota(jnp.int32, (tile, LANES), 1)
    first = lane < SB_HEAD_DIM
    heads = []
    for p in range(blocks):
        q = q_ref[:, p * LANES:(p + 1) * LANES]
        zero_q = jnp.zeros_like(q)
        heads += [(p, jnp.where(first, q, zero_q)), (p, jnp.where(first, zero_q, q))]
    row = lax.broadcasted_iota(jnp.int32, (tile, tile), 0)
    col = lax.broadcasted_iota(jnp.int32, (tile, tile), 1)
    neg_later = jnp.where(row > col, -1.0, 0.0).astype(BF16)
    causal = col < row

    def key_rows(j):
        return pl.ds(pl.multiple_of(j * tile, tile), tile)

    def scores(j, masked):
        for h, (p, qm) in enumerate(heads):
            kt = k_ref[key_rows(j), p * LANES:(p + 1) * LANES]
            z = lax.dot_general(qm, kt, (((1,), (1,)), ((), ())), preferred_element_type=F32)
            sp = jnp.maximum(z, 0.0) + jnp.log(1.0 + jnp.exp(-jnp.abs(z)))
            log_beta = z - sp
            if masked:
                sp = jnp.where(causal, sp, 0.0)
                log_beta = jnp.where(causal, log_beta, MASKED_LOG_WEIGHT)
            sp_ref[h] = sp.astype(BF16)
            lb_ref[h] = log_beta

    def weights(carries):
        new = []
        for h in range(len(heads)):
            sp = sp_ref[h]
            r = _dot(sp, neg_later) + carries[h]
            w_ref[h] = jnp.exp(lb_ref[h] + r).astype(BF16)
            new.append(r[:, 0:1] - sp[:, 0:1].astype(F32))
        return tuple(new)

    def accumulate(j):
        for h, (p, _) in enumerate(heads):
            acc_ref[h] += _dot(w_ref[h], v_ref[key_rows(j), p * LANES:(p + 1) * LANES])

    acc_ref[...] = jnp.zeros_like(acc_ref)
    w_ref[...] = jnp.zeros_like(w_ref)
    scores(qi, True)

    def trip(it, carries):
        j = qi - it
        accumulate(jnp.minimum(j + 1, qi))
        carries = weights(carries)
        scores(j - 1, False)
        return carries

    carries = lax.fori_loop(0, qi, trip, tuple(jnp.zeros((tile, 1), F32) for _ in heads))
    accumulate(jnp.minimum(1, qi))
    weights(carries)
    accumulate(0)
    for p in range(blocks):
        o_ref[:, p * LANES:(p + 1) * LANES] = jnp.where(first, acc_ref[2 * p], acc_ref[2 * p + 1]).astype(o_ref.dtype)


def _attn(q, k, v, *, seq, tile=256, blocks=4):
    t, width = q.shape
    batch = t // seq
    groups = width // (blocks * LANES)
    q_tiles = seq // tile
    qo = pl.BlockSpec((tile, blocks * LANES), lambda b, p, i: (b * q_tiles + i, p))
    kv = pl.BlockSpec((seq, blocks * LANES), lambda b, p, i: (b, p))
    return pl.pallas_call(
        functools.partial(_attn_body, tile=tile, blocks=blocks),
        grid=(batch, groups, q_tiles),
        in_specs=[qo, kv, kv],
        out_specs=qo,
        out_shape=jax.ShapeDtypeStruct((t, width), BF16),
        scratch_shapes=[pltpu.VMEM((2 * blocks, tile, LANES), F32),
                        pltpu.VMEM((2 * blocks, tile, tile), BF16),
                        pltpu.VMEM((2 * blocks, tile, tile), F32),
                        pltpu.VMEM((2 * blocks, tile, tile), BF16)],
        compiler_params=_params(("arbitrary", "arbitrary", "arbitrary")),
        name="stickbreak",
    )(q, k, v)


def _mixout_body(x_ref, ya_ref, yb_ref, wglu_ref, wout_ref, o_ref):
    y = ya_ref[...]
    width = y.shape[1]
    z = 0.5 * y * (1.0 + jnp.tanh(math.sqrt(2.0 / math.pi) * (y + 0.044715 * (y * y * y))))
    gate = _dot(z.astype(BF16), wglu_ref[...])
    ya = (z * jax.nn.sigmoid(gate)).astype(BF16)
    o_ref[...] = x_ref[...] + _dot(ya, wout_ref[:width, :]) + _dot(yb_ref[...], wout_ref[width:, :])


def _mixout(x, ya, yb, wglu, wout, *, tm=512):
    t, d = x.shape
    width = ya.shape[1]
    row = pl.BlockSpec((tm, d), lambda i: (i, 0))
    half = pl.BlockSpec((tm, width), lambda i: (i, 0))
    return pl.pallas_call(
        _mixout_body,
        grid=(t // tm,),
        in_specs=[row, half, half, _resident(wglu.shape), _resident(wout.shape)],
        out_specs=row,
        out_shape=jax.ShapeDtypeStruct((t, d), F32),
        compiler_params=_params(("arbitrary",)),
        name="mixout0",
    )(x, ya, yb, wglu, wout)


def _conv_body(x_ref, g_ref, win_ref, cw_ref, wout_ref, o_ref, prev_ref, *, tiles_per_seq, ch):
    i = pl.program_id(0)

    @pl.when(i % tiles_per_seq == 0)
    def _():
        prev_ref[...] = jnp.zeros_like(prev_ref)

    x = x_ref[...]
    tm = x.shape[0]
    h = _rms(x, g_ref[...]).astype(BF16)
    p = _dot(h, win_ref[...])
    b_gate = p[:, :ch]
    cv = p[:, ch:2 * ch] * p[:, 2 * ch:]
    prev = prev_ref[...]
    row = lax.broadcasted_iota(jnp.int32, (tm, 1), 0)
    back1 = jnp.where(row == 0, prev[7:8, :], pltpu.roll(cv, 1, 0))
    back2 = jnp.where(row == 0, prev[6:7, :], jnp.where(row == 1, prev[7:8, :], pltpu.roll(cv, 2, 0)))
    y = cw_ref[0:1, :] * back2 + cw_ref[1:2, :] * back1 + cw_ref[2:3, :] * cv
    prev_ref[...] = cv[tm - 8:, :]
    o_ref[...] = x + _dot((b_gate * y).astype(BF16), wout_ref[...])


def _conv_mixer(x, g, win, cw, wout, *, seq, tm=512):
    t, d = x.shape
    ch = wout.shape[0]
    row = pl.BlockSpec((tm, d), lambda i: (i, 0))
    return pl.pallas_call(
        functools.partial(_conv_body, tiles_per_seq=seq // tm, ch=ch),
        grid=(t // tm,),
        in_specs=[row, _resident((1, d)), _resident(win.shape), _resident(cw.shape), _resident(wout.shape)],
        out_specs=row,
        out_shape=jax.ShapeDtypeStruct((t, d), F32),
        scratch_shapes=[pltpu.VMEM((8, ch), F32)],
        compiler_params=_params(("arbitrary",)),
        name="conv_mixer",
    )(x, g, win, cw, wout)


def _pad_ffn(wg, wu, wd):
    f = wg.shape[2]
    pad = -f % D_FF_PAD_MULTIPLE
    return (jnp.pad(wg.astype(BF16), ((0, 0), (0, 0), (0, pad))), jnp.pad(wu.astype(BF16), ((0, 0), (0, 0), (0, pad))),
            jnp.pad(wd.astype(BF16), ((0, 0), (0, pad), (0, 0))))


def kernel(x, ffn1_norm, ffn1_w_gate, ffn1_w_up, ffn1_w_down, mix_norm, ffn2_norm, ffn2_w_gate, ffn2_w_up,
           ffn2_w_down, ab_w_in, s5_lambda_re, s5_lambda_im, s5_log_dt, s5_b_re, s5_b_im, s5_c_re, s5_c_im, s5_d,
           s5_w_glu, ab_w_out, sc_w_in, sc_conv_w, sc_w_out, final_norm):
    batch, seq, d = x.shape
    depth = ffn1_norm.shape[0]
    xt = x.reshape(batch * seq, d)
    ffn1 = _pad_ffn(ffn1_w_gate, ffn1_w_up, ffn1_w_down)
    ffn2 = _pad_ffn(ffn2_w_gate, ffn2_w_up, ffn2_w_down)
    for layer in range(depth):
        xt = _ffn(xt, ffn1_norm[layer][None], ffn1, layer)
        g_mix = mix_norm[layer][None]
        if layer % 2 == 0:
            e = layer // 2
            u, q, k, v = _proj0(xt, g_mix, ab_w_in[e].astype(BF16))
            tables = _s5_tables(s5_lambda_re[e], s5_lambda_im[e], s5_log_dt[e], s5_b_re[e], s5_b_im[e],
                                s5_c_re[e], s5_c_im[e], s5_d[e])
            ya = _s5(u, tables, seq=seq)
            yb = _attn(q, k, v, seq=seq)
            xt = _mixout(xt, ya, yb, s5_w_glu[e].astype(BF16), ab_w_out[e].astype(BF16))
        else:
            o = layer // 2
            xt = _conv_mixer(xt, g_mix, sc_w_in[o].astype(BF16), sc_conv_w[o], sc_w_out[o].astype(BF16), seq=seq)
        last = layer == depth - 1
        xt = _ffn(xt, ffn2_norm[layer][None], ffn2, layer, final_g=final_norm[None] if last else None)
    return xt.reshape(batch, seq, d)
```

```python
import functools
import math

import jax
import jax.numpy as jnp
from jax import lax
from jax.experimental import pallas as pl
from jax.experimental.pallas import tpu as pltpu

F32 = jnp.float32
BF16 = jnp.bfloat16

EPS = 1e-6
LANES = 128
D_FF_PAD_MULTIPLE = 128
S5_GROUP = 16
S5_STATE = 64
S5_CHUNK = 16
GROUPS_PER_BLOCK = LANES // S5_GROUP
SB_HEAD_DIM = 64
LOG2_E = math.log2(math.e)
MASKED_LOG_WEIGHT = -1e30
CONV_K = 3
VMEM_LIMIT_BYTES = 56 * 1024 * 1024


def _params(semantics):
    return pltpu.CompilerParams(dimension_semantics=semantics, vmem_limit_bytes=VMEM_LIMIT_BYTES)


def _resident(shape):
    zeros = (0,) * len(shape)
    return pl.BlockSpec(shape, lambda *_: zeros, pipeline_mode=pl.Buffered(1))


def _rms(x, g):
    r = lax.rsqrt(jnp.mean(x * x, axis=-1, keepdims=True) + EPS)
    return x * r * g


def _dot(a, b):
    return jnp.dot(a, b, preferred_element_type=F32)


def _ffn_body(x_ref, g_ref, wg_ref, wu_ref, wd_ref, *rest, final_norm):
    if final_norm:
        fg_ref, o_ref = rest
    else:
        (o_ref,) = rest
    x = x_ref[...]
    h = _rms(x, g_ref[...]).astype(BF16)
    gate = _dot(h, wg_ref[...])
    up = _dot(h, wu_ref[...])
    act = (gate * jax.nn.sigmoid(gate) * up).astype(BF16)
    y = x + 0.5 * _dot(act, wd_ref[...])
    if final_norm:
        y = _rms(y, fg_ref[...])
    o_ref[...] = y


def _ffn(x, g, weights, layer, final_g=None, *, tm=512):
    t, d = x.shape
    wg, wu, wd = weights
    f = wg.shape[2]
    row = pl.BlockSpec((tm, d), lambda i: (i, 0))

    def of_layer(shape):
        return pl.BlockSpec((None,) + shape, lambda i: (layer, 0, 0), pipeline_mode=pl.Buffered(1))

    in_specs = [row, _resident((1, d)), of_layer((d, f)), of_layer((d, f)), of_layer((f, d))]
    args = [x, g, wg, wu, wd]
    if final_g is not None:
        in_specs.append(_resident((1, d)))
        args.append(final_g)
    return pl.pallas_call(
        functools.partial(_ffn_body, final_norm=final_g is not None),
        grid=(t // tm,),
        in_specs=in_specs,
        out_specs=row,
        out_shape=jax.ShapeDtypeStruct((t, d), F32),
        compiler_params=_params(("arbitrary",)),
        name="ffn",
    )(*args)


def _proj0_body(x_ref, g_ref, w_ref, u_ref, q_ref, k_ref, v_ref, *, width, scale):
    h = _rms(x_ref[...], g_ref[...]).astype(BF16)
    p = _dot(h, w_ref[...])
    u_ref[...] = p[:, :width]
    q_ref[...] = (p[:, width:2 * width] * scale).astype(BF16)
    k_ref[...] = p[:, 2 * width:3 * width].astype(BF16)
    v_ref[...] = p[:, 3 * width:].astype(BF16)


def _proj0(x, g, w, *, tm=512):
    t, d = x.shape
    width = w.shape[1] // 4
    row = pl.BlockSpec((tm, d), lambda i: (i, 0))
    out = pl.BlockSpec((tm, width), lambda i: (i, 0))
    return pl.pallas_call(
        functools.partial(_proj0_body, width=width, scale=1.0 / math.sqrt(SB_HEAD_DIM)),
        grid=(t // tm,),
        in_specs=[row, _resident((1, d)), _resident(w.shape)],
        out_specs=[out, out, out, out],
        out_shape=[jax.ShapeDtypeStruct((t, width), F32)] + [jax.ShapeDtypeStruct((t, width), BF16)] * 3,
        compiler_params=_params(("arbitrary",)),
        name="proj0",
    )(x, g, w)


def _s5_tables(lam_re, lam_im, log_dt, b_re, b_im, c_re, c_im, d):
    hi = lax.Precision.HIGHEST
    g_all, n_st = lam_re.shape
    p_ch = b_re.shape[-1]
    c = S5_CHUNK
    dt = jnp.exp(log_dt)[:, None]
    mag = jnp.exp(lam_re * dt)
    ab_re = mag * jnp.cos(lam_im * dt)
    ab_im = mag * jnp.sin(lam_im * dt)
    den = lam_re * lam_re + lam_im * lam_im
    nr = ab_re - 1.0
    coef_re = (nr * lam_re + ab_im * lam_im) / den
    coef_im = (ab_im * lam_re - nr * lam_im) / den
    bb_re = coef_re[..., None] * b_re - coef_im[..., None] * b_im
    bb_im = coef_re[..., None] * b_im + coef_im[..., None] * b_re
    pw_re = [jnp.ones_like(ab_re)]
    pw_im = [jnp.zeros_like(ab_im)]
    for _ in range(c):
        pr, pi = pw_re[-1], pw_im[-1]
        pw_re.append(pr * ab_re - pi * ab_im)
        pw_im.append(pr * ab_im + pi * ab_re)
    pw_re = jnp.stack(pw_re)
    pw_im = jnp.stack(pw_im)
    ab_b_re = pw_re[..., None] * bb_re - pw_im[..., None] * bb_im
    ab_b_im = pw_re[..., None] * bb_im + pw_im[..., None] * bb_re
    kern = (jnp.einsum('tgnp,gqn->tgpq', ab_b_re[:c], c_re, precision=hi)
            - jnp.einsum('tgnp,gqn->tgpq', ab_b_im[:c], c_im, precision=hi))
    kern = kern.at[0].add(jax.vmap(jnp.diag)(d.reshape(g_all, p_ch)))
    gpb = GROUPS_PER_BLOCK
    j_blocks = g_all // gpb

    kern_pad = jnp.concatenate([jnp.zeros_like(kern[:1]), kern], axis=0)
    lag = (c - 1) + jnp.arange(2)[None, :] - jnp.arange(c)[:, None]
    sel = kern_pad[lag].reshape(c, 2, j_blocks, gpb, p_ch, p_ch)
    master_c = jnp.transpose(sel, (2, 0, 3, 4, 1, 5)).reshape(j_blocks, c * LANES, 2 * p_ch)
    st = jnp.stack([ab_b_re[:c][::-1], ab_b_im[:c][::-1]])
    st = st.reshape(2, c, j_blocks, gpb, n_st, p_ch)
    state_c = jnp.transpose(st, (2, 1, 3, 5, 0, 4)).reshape(j_blocks, c * LANES, 2 * n_st)
    q_re = c_re[None] * pw_re[1:, :, None, :] - c_im[None] * pw_im[1:, :, None, :]
    q_im = c_re[None] * pw_im[1:, :, None, :] + c_im[None] * pw_re[1:, :, None, :]
    ro = jnp.stack([q_re, -q_im]).reshape(2, c, j_blocks, gpb, p_ch, n_st)
    out_c = jnp.transpose(ro, (2, 0, 3, 5, 1, 4)).reshape(j_blocks, 2 * gpb * n_st, c * p_ch)
    a_pow = jnp.stack([pw_re[c].reshape(j_blocks, gpb * n_st), pw_im[c].reshape(j_blocks, gpb * n_st)], axis=1)
    return master_c.astype(BF16), state_c.astype(BF16), out_c.astype(BF16), a_pow


def _log2(n):
    assert n > 0 and n & (n - 1) == 0, n
    return n.bit_length() - 1


def _expand_block_diag(dst_ref, src_ref, *, inner, row_div, col_div, row_chunk=256):
    rows, k = src_ref.shape
    cols = dst_ref.shape[1]
    block = cols // (k // inner)
    kk = lax.broadcasted_iota(jnp.int32, (k, cols), 0)
    kc = lax.broadcasted_iota(jnp.int32, (k, cols), 1)
    spread = (((kk >> _log2(inner)) == (kc >> _log2(block)))
              & ((kk & (inner - 1)) == (kc & (inner - 1)))).astype(BF16)
    row = lax.broadcasted_iota(jnp.int32, (row_chunk, cols), 0)
    col = lax.broadcasted_iota(jnp.int32, (row_chunk, cols), 1)
    col_group = (col >> _log2(col_div)) & (GROUPS_PER_BLOCK - 1)
    for r0 in range(0, rows, row_chunk):
        keep = (((row + r0) >> _log2(row_div)) & (GROUPS_PER_BLOCK - 1)) == col_group
        tiled = _dot(src_ref[r0:r0 + row_chunk, :], spread)
        dst_ref[r0:r0 + row_chunk, :] = jnp.where(keep, tiled, 0.0).astype(dst_ref.dtype)


def _s5_body(u_ref, mc_ref, sc_ref, oc_ref, ap_ref, y_ref, master_ref, ws_ref, wo_ref, s_ref, hp_ref, *, chunks):
    c = S5_CHUNK
    half = hp_ref.shape[1] // 2

    @pl.when(pl.program_id(1) == 0)
    def _():
        _expand_block_diag(master_ref, mc_ref, inner=S5_GROUP, row_div=S5_GROUP, col_div=S5_GROUP)
        _expand_block_diag(ws_ref, sc_ref, inner=S5_STATE, row_div=S5_GROUP, col_div=S5_STATE)
        _expand_block_diag(wo_ref, oc_ref, inner=S5_GROUP, row_div=S5_STATE, col_div=S5_GROUP)

    u_flat = jnp.concatenate(
        [u_ref[pl.ds(t, chunks, stride=c), :].astype(BF16) for t in range(c)], axis=1)
    s_ref[...] = _dot(u_flat, ws_ref[...])
    a_re = ap_ref[0:1, :]
    a_im = ap_ref[1:2, :]

    def step(k, carry):
        h_re, h_im = carry
        hp_ref[pl.ds(k, 1), :half] = h_re
        hp_ref[pl.ds(k, 1), half:] = h_im
        s = s_ref[pl.ds(k, 1), :]
        return (a_re * h_re - a_im * h_im + s[:, :half], a_re * h_im + a_im * h_re + s[:, half:])

    zero = jnp.zeros((1, half), F32)
    lax.fori_loop(0, chunks, step, (zero, zero), unroll=8)
    h_prev = hp_ref[...].astype(BF16)
    for tp in range(0, c, 2):
        yp = (_dot(u_flat[:, :LANES * (tp + 2)], master_ref[(c - 2 - tp) * LANES:, :])
              + _dot(h_prev, wo_ref[:, tp * LANES:(tp + 2) * LANES]))
        y_ref[pl.ds(tp, chunks, stride=c), :] = yp[:, :LANES]
        y_ref[pl.ds(tp + 1, chunks, stride=c), :] = yp[:, LANES:]


def _s5(u, tables, *, seq):
    t, width = u.shape
    master_c, state_c, out_c, a_pow = tables
    j_blocks = width // LANES
    batch = t // seq
    chunks = seq // S5_CHUNK
    n_state = out_c.shape[1]
    rows = S5_CHUNK * LANES
    io = pl.BlockSpec((seq, LANES), lambda j, b: (b, j))

    def table(a):
        return pl.BlockSpec((None,) + a.shape[1:], lambda j, b: (j, 0, 0))

    return pl.pallas_call(
        functools.partial(_s5_body, chunks=chunks),
        grid=(j_blocks, batch),
        in_specs=[io, table(master_c), table(state_c), table(out_c), table(a_pow)],
        out_specs=io,
        out_shape=jax.ShapeDtypeStruct((t, width), F32),
        scratch_shapes=[pltpu.VMEM((rows, 2 * LANES), BF16),
                        pltpu.VMEM((rows, n_state), BF16),
                        pltpu.VMEM((n_state, rows), BF16),
                        pltpu.VMEM((chunks, n_state), F32),
                        pltpu.VMEM((chunks, n_state), F32)],
        compiler_params=_params(("arbitrary", "arbitrary")),
        name="s5",
    )(u, master_c, state_c, out_c, a_pow)


def _attn_body(q_ref, k_ref, v_ref, o_ref, acc_ref, carry_ref, z_ref, sp_ref, lb_ref, w_ref, *, tile, blocks):
    n_heads = 2 * blocks
    nq = acc_ref.shape[0]
    n_items = nq * (nq + 1) // 2
    lane = lax.broadcasted_iota(jnp.int32, (tile, LANES), 1)
    first = lane < SB_HEAD_DIM
    row = lax.broadcasted_iota(jnp.int32, (tile, tile), 0)
    col = lax.broadcasted_iota(jnp.int32, (tile, tile), 1)
    neg_later = jnp.where(row > col, -1.0, 0.0).astype(BF16)
    causal = col < row

    def tile_rows(j):
        return pl.ds(pl.multiple_of(j * tile, tile), tile)

    def scores(qi, j):
        for p in range(blocks):
            lanes = slice(p * LANES, (p + 1) * LANES)
            q = q_ref[tile_rows(qi), lanes]
            kt = k_ref[tile_rows(j), lanes]
            zero_q = jnp.zeros_like(q)
            for h, qm in ((2 * p, jnp.where(first, q, zero_q)), (2 * p + 1, jnp.where(first, zero_q, q))):
                z_ref[h] = lax.dot_general(qm, kt, (((1,), (1,)), ((), ())), preferred_element_type=F32)

    def log_terms(masked):
        for h in range(n_heads):
            z = z_ref[h]
            sp = jnp.maximum(z, 0.0) + jnp.log(1.0 + jnp.exp2(jnp.abs(z) * -LOG2_E))
            log_beta = z - sp
            if masked:
                sp = jnp.where(causal, sp, 0.0)
                log_beta = jnp.where(causal, log_beta, MASKED_LOG_WEIGHT)
            sp_ref[h] = sp.astype(BF16)
            lb_ref[h] = log_beta

    def weights(qi):
        for h in range(n_heads):
            sp = sp_ref[h]
            carry = carry_ref[qi, h]
            r = _dot(sp, neg_later) + jnp.concatenate([carry] * (tile // LANES), axis=1)
            w_ref[h] = jnp.exp(lb_ref[h] + r).astype(BF16)
            carry_ref[qi, h] = jnp.broadcast_to(r[:, 0:1] - sp[:, 0:1].astype(F32), (tile, LANES))

    def accumulate(qi, j):
        for h in range(n_heads):
            lanes = slice((h // 2) * LANES, (h // 2 + 1) * LANES)
            acc_ref[qi, h] += _dot(w_ref[h], v_ref[tile_rows(j), lanes])

    def next_item(qi, d):
        wrap = qi + 1 == nq
        last = jnp.logical_and(wrap, d + 1 == nq)
        nd = jnp.where(jnp.logical_and(wrap, jnp.logical_not(last)), d + 1, d)
        nqi = jnp.where(wrap, jnp.where(last, qi, d + 1), qi + 1)
        return nqi, nd

    acc_ref[...] = jnp.zeros_like(acc_ref)
    carry_ref[...] = jnp.zeros_like(carry_ref)
    w_ref[...] = jnp.zeros_like(w_ref)
    zero = jnp.int32(0)
    scores(zero, zero)
    log_terms(True)
    qi1, d1 = next_item(zero, zero)
    scores(qi1, qi1 - d1)

    def make_trip(masked):
        def trip(_, state):
            (qi_z, d_z), (qi_l, j_l), (qi_w, j_w) = state
            accumulate(qi_w, j_w)
            weights(qi_l)
            log_terms(masked)
            nqi, nd = next_item(qi_z, d_z)
            scores(nqi, nqi - nd)
            return (nqi, nd), (qi_z, qi_z - d_z), (qi_l, j_l)
        return trip

    state = ((qi1, d1), (zero, zero), (zero, zero))
    state = lax.fori_loop(2, nq + 1, make_trip(True), state)
    lax.fori_loop(nq + 1, n_items + 3, make_trip(False), state)
    for qi in range(nq):
        for p in range(blocks):
            o_ref[qi * tile:(qi + 1) * tile, p * LANES:(p + 1) * LANES] = jnp.where(
                first, acc_ref[qi, 2 * p], acc_ref[qi, 2 * p + 1]).astype(o_ref.dtype)


def _attn(q, k, v, *, seq, tile=256, blocks=2):
    t, width = q.shape
    batch = t // seq
    groups = width // (blocks * LANES)
    q_tiles = seq // tile
    io = pl.BlockSpec((seq, blocks * LANES), lambda b, p: (b, p))
    return pl.pallas_call(
        functools.partial(_attn_body, tile=tile, blocks=blocks),
        grid=(batch, groups),
        in_specs=[io, io, io],
        out_specs=io,
        out_shape=jax.ShapeDtypeStruct((t, width), BF16),
        scratch_shapes=[pltpu.VMEM((q_tiles, 2 * blocks, tile, LANES), F32),
                        pltpu.VMEM((q_tiles, 2 * blocks, tile, LANES), F32),
                        pltpu.VMEM((2 * blocks, tile, tile), F32),
                        pltpu.VMEM((2 * blocks, tile, tile), BF16),
                        pltpu.VMEM((2 * blocks, tile, tile), F32),
                        pltpu.VMEM((2 * blocks, tile, tile), BF16)],
        compiler_params=_params(("arbitrary", "arbitrary")),
        name="stickbreak",
    )(q, k, v)


def _mixout_body(x_ref, ya_ref, yb_ref, wglu_ref, wout_ref, o_ref):
    y = ya_ref[...]
    width = y.shape[1]
    z = 0.5 * y * (1.0 + jnp.tanh(math.sqrt(2.0 / math.pi) * (y + 0.044715 * (y * y * y))))
    gate = _dot(z.astype(BF16), wglu_ref[...])
    ya = (z * jax.nn.sigmoid(gate)).astype(BF16)
    o_ref[...] = x_ref[...] + _dot(ya, wout_ref[:width, :]) + _dot(yb_ref[...], wout_ref[width:, :])


def _mixout(x, ya, yb, wglu, wout, *, tm=512):
    t, d = x.shape
    width = ya.shape[1]
    row = pl.BlockSpec((tm, d), lambda i: (i, 0))
    half = pl.BlockSpec((tm, width), lambda i: (i, 0))
    return pl.pallas_call(
        _mixout_body,
        grid=(t // tm,),
        in_specs=[row, half, half, _resident(wglu.shape), _resident(wout.shape)],
        out_specs=row,
        out_shape=jax.ShapeDtypeStruct((t, d), F32),
        compiler_params=_params(("arbitrary",)),
        name="mixout0",
    )(x, ya, yb, wglu, wout)


def _conv_body(x_ref, g_ref, win_ref, cw_ref, wout_ref, o_ref, prev_ref, *, tiles_per_seq, ch):
    i = pl.program_id(0)

    @pl.when(i % tiles_per_seq == 0)
    def _():
        prev_ref[...] = jnp.zeros_like(prev_ref)

    x = x_ref[...]
    tm = x.shape[0]
    h = _rms(x, g_ref[...]).astype(BF16)
    p = _dot(h, win_ref[...])
    b_gate = p[:, :ch]
    cv = p[:, ch:2 * ch] * p[:, 2 * ch:]
    prev = prev_ref[...]
    row = lax.broadcasted_iota(jnp.int32, (tm, 1), 0)
    back1 = jnp.where(row == 0, prev[7:8, :], pltpu.roll(cv, 1, 0))
    back2 = jnp.where(row == 0, prev[6:7, :], jnp.where(row == 1, prev[7:8, :], pltpu.roll(cv, 2, 0)))
    y = cw_ref[0:1, :] * back2 + cw_ref[1:2, :] * back1 + cw_ref[2:3, :] * cv
    prev_ref[...] = cv[tm - 8:, :]
    o_ref[...] = x + _dot((b_gate * y).astype(BF16), wout_ref[...])


def _conv_mixer(x, g, win, cw, wout, *, seq, tm=512):
    t, d = x.shape
    ch = wout.shape[0]
    row = pl.BlockSpec((tm, d), lambda i: (i, 0))
    return pl.pallas_call(
        functools.partial(_conv_body, tiles_per_seq=seq // tm, ch=ch),
        grid=(t // tm,),
        in_specs=[row, _resident((1, d)), _resident(win.shape), _resident(cw.shape), _resident(wout.shape)],
        out_specs=row,
        out_shape=jax.ShapeDtypeStruct((t, d), F32),
        scratch_shapes=[pltpu.VMEM((8, ch), F32)],
        compiler_params=_params(("arbitrary",)),
        name="conv_mixer",
    )(x, g, win, cw, wout)


def _pad_ffn(wg, wu, wd):
    f = wg.shape[2]
    pad = -f % D_FF_PAD_MULTIPLE
    return (jnp.pad(wg.astype(BF16), ((0, 0), (0, 0), (0, pad))), jnp.pad(wu.astype(BF16), ((0, 0), (0, 0), (0, pad))),
            jnp.pad(wd.astype(BF16), ((0, 0), (0, pad), (0, 0))))


def kernel(x, ffn1_norm, ffn1_w_gate, ffn1_w_up, ffn1_w_down, mix_norm, ffn2_norm, ffn2_w_gate, ffn2_w_up,
           ffn2_w_down, ab_w_in, s5_lambda_re, s5_lambda_im, s5_log_dt, s5_b_re, s5_b_im, s5_c_re, s5_c_im, s5_d,
           s5_w_glu, ab_w_out, sc_w_in, sc_conv_w, sc_w_out, final_norm):
    batch, seq, d = x.shape
    depth = ffn1_norm.shape[0]
    xt = x.reshape(batch * seq, d)
    ffn1 = _pad_ffn(ffn1_w_gate, ffn1_w_up, ffn1_w_down)
    ffn2 = _pad_ffn(ffn2_w_gate, ffn2_w_up, ffn2_w_down)
    for layer in range(depth):
        xt = _ffn(xt, ffn1_norm[layer][None], ffn1, layer)
        g_mix = mix_norm[layer][None]
        if layer % 2 == 0:
            e = layer // 2
            u, q, k, v = _proj0(xt, g_mix, ab_w_in[e].astype(BF16))
            tables = _s5_tables(s5_lambda_re[e], s5_lambda_im[e], s5_log_dt[e], s5_b_re[e], s5_b_im[e],
                                s5_c_re[e], s5_c_im[e], s5_d[e])
            ya = _s5(u, tables, seq=seq)
            yb = _attn(q, k, v, seq=seq)
            xt = _mixout(xt, ya, yb, s5_w_glu[e].astype(BF16), ab_w_out[e].astype(BF16))
        else:
            o = layer // 2
            xt = _conv_mixer(xt, g_mix, sc_w_in[o].astype(BF16), sc_conv_w[o], sc_w_out[o].astype(BF16), seq=seq)
        last = layer == depth - 1
        xt = _ffn(xt, ffn2_norm[layer][None], ffn2, layer, final_g=final_norm[None] if last else None)
    return xt.reshape(batch, seq, d)
```

```python
import functools
import math

import jax
import jax.numpy as jnp
from jax import lax
from jax.experimental import pallas as pl
from jax.experimental.pallas import tpu as pltpu

F32 = jnp.float32
BF16 = jnp.bfloat16

EPS = 1e-6
LANES = 128
D_FF_PAD_MULTIPLE = 128
S5_GROUP = 16
S5_STATE = 64
S5_CHUNK = 16
GROUPS_PER_BLOCK = LANES // S5_GROUP
SB_HEAD_DIM = 64
LOG2_E = math.log2(math.e)
MASKED_LOG_WEIGHT = -1e30
CONV_K = 3
VMEM_LIMIT_BYTES = 56 * 1024 * 1024


def _params(semantics):
    return pltpu.CompilerParams(dimension_semantics=semantics, vmem_limit_bytes=VMEM_LIMIT_BYTES)


def _resident(shape):
    zeros = (0,) * len(shape)
    return pl.BlockSpec(shape, lambda *_: zeros, pipeline_mode=pl.Buffered(1))


def _rms(x, g):
    r = lax.rsqrt(jnp.mean(x * x, axis=-1, keepdims=True) + EPS)
    return x * r * g


def _dot(a, b):
    return jnp.dot(a, b, preferred_element_type=F32)


def _dot_nt(a, b):
    return lax.dot_general(a, b, (((1,), (1,)), ((), ())), preferred_element_type=F32)


def _cast_pad_body(w_ref, o_ref, *, axis):
    n = w_ref.shape[axis]
    extra = o_ref.shape[axis] - n
    w = w_ref[...].astype(BF16)
    if axis == 1:
        o_ref[:, :n] = w
        o_ref[:, n:] = jnp.zeros((o_ref.shape[0], extra), BF16)
    else:
        o_ref[:n, :] = w
        o_ref[n:, :] = jnp.zeros((extra, o_ref.shape[1]), BF16)


def _cast_pad(w, *, axis, block=256):
    layers, a, b = w.shape
    n = w.shape[axis]
    n_pad = n + (-n % D_FF_PAD_MULTIPLE)
    if axis == 2:
        in_block, out_block, out_shape = (None, block, b), (None, block, n_pad), (layers, a, n_pad)
        index = lambda l, i: (l, i, 0)
        steps = a // block
    else:
        in_block, out_block, out_shape = (None, a, block), (None, n_pad, block), (layers, n_pad, b)
        index = lambda l, i: (l, 0, i)
        steps = b // block
    return pl.pallas_call(
        functools.partial(_cast_pad_body, axis=axis - 1),
        grid=(layers, steps),
        in_specs=[pl.BlockSpec(in_block, index)],
        out_specs=pl.BlockSpec(out_block, index),
        out_shape=jax.ShapeDtypeStruct(out_shape, BF16),
        compiler_params=_params(("arbitrary", "arbitrary")),
        name="cast_pad",
    )(w)


def _ffn_body(x_ref, g_ref, wg_ref, wu_ref, wd_ref, *rest, final_norm):
    if final_norm:
        fg_ref, o_ref = rest
    else:
        (o_ref,) = rest
    x = x_ref[...]
    h = _rms(x, g_ref[...]).astype(BF16)
    gate = _dot(h, wg_ref[...])
    up = _dot(h, wu_ref[...])
    act = (gate * jax.nn.sigmoid(gate) * up).astype(BF16)
    y = x + 0.5 * _dot(act, wd_ref[...])
    if final_norm:
        y = _rms(y, fg_ref[...])
    o_ref[...] = y


def _ffn(x, g, weights, layer, final_g=None, *, tm=512):
    t, d = x.shape
    wg, wu, wd = weights
    f = wg.shape[2]
    row = pl.BlockSpec((tm, d), lambda i: (i, 0))

    def of_layer(shape):
        return pl.BlockSpec((None,) + shape, lambda i: (layer, 0, 0), pipeline_mode=pl.Buffered(1))

    in_specs = [row, _resident((1, d)), of_layer((d, f)), of_layer((d, f)), of_layer((f, d))]
    args = [x, g, wg, wu, wd]
    if final_g is not None:
        in_specs.append(_resident((1, d)))
        args.append(final_g)
    return pl.pallas_call(
        functools.partial(_ffn_body, final_norm=final_g is not None),
        grid=(t // tm,),
        in_specs=in_specs,
        out_specs=row,
        out_shape=jax.ShapeDtypeStruct((t, d), F32),
        compiler_params=_params(("arbitrary",)),
        name="ffn",
    )(*args)


def _proj0_body(x_ref, g_ref, w_ref, u_ref, q_ref, k_ref, v_ref, *, width, scale):
    h = _rms(x_ref[...], g_ref[...]).astype(BF16)
    p = _dot(h, w_ref[...])
    u_ref[...] = p[:, :width]
    q_ref[...] = (p[:, width:2 * width] * scale).astype(BF16)
    k_ref[...] = p[:, 2 * width:3 * width].astype(BF16)
    v_ref[...] = p[:, 3 * width:].astype(BF16)


def _proj0(x, g, w, *, tm=512):
    t, d = x.shape
    width = w.shape[1] // 4
    row = pl.BlockSpec((tm, d), lambda i: (i, 0))
    out = pl.BlockSpec((tm, width), lambda i: (i, 0))
    return pl.pallas_call(
        functools.partial(_proj0_body, width=width, scale=1.0 / math.sqrt(SB_HEAD_DIM)),
        grid=(t // tm,),
        in_specs=[row, _resident((1, d)), _resident(w.shape)],
        out_specs=[out, out, out, out],
        out_shape=[jax.ShapeDtypeStruct((t, width), F32)] + [jax.ShapeDtypeStruct((t, width), BF16)] * 3,
        compiler_params=_params(("arbitrary",)),
        name="proj0",
    )(x, g, w)


def _s5_tables(lam_re, lam_im, log_dt, b_re, b_im, c_re, c_im, d):
    hi = lax.Precision.HIGHEST
    g_all, n_st = lam_re.shape
    p_ch = b_re.shape[-1]
    c = S5_CHUNK
    dt = jnp.exp(log_dt)[:, None]
    mag = jnp.exp(lam_re * dt)
    ab_re = mag * jnp.cos(lam_im * dt)
    ab_im = mag * jnp.sin(lam_im * dt)
    den = lam_re * lam_re + lam_im * lam_im
    nr = ab_re - 1.0
    coef_re = (nr * lam_re + ab_im * lam_im) / den
    coef_im = (ab_im * lam_re - nr * lam_im) / den
    b_re_t = jnp.swapaxes(b_re, -1, -2)
    b_im_t = jnp.swapaxes(b_im, -1, -2)
    bb_re = coef_re[:, None, :] * b_re_t - coef_im[:, None, :] * b_im_t
    bb_im = coef_re[:, None, :] * b_im_t + coef_im[:, None, :] * b_re_t
    pw_re = [jnp.ones_like(ab_re)]
    pw_im = [jnp.zeros_like(ab_im)]
    for _ in range(c):
        pr, pi = pw_re[-1], pw_im[-1]
        pw_re.append(pr * ab_re - pi * ab_im)
        pw_im.append(pr * ab_im + pi * ab_re)
    pw_re = jnp.stack(pw_re)[:, :, None, :]
    pw_im = jnp.stack(pw_im)[:, :, None, :]
    ab_b_re = pw_re[:c] * bb_re - pw_im[:c] * bb_im
    ab_b_im = pw_re[:c] * bb_im + pw_im[:c] * bb_re
    kern = (jnp.einsum('tgpn,gqn->tgpq', ab_b_re, c_re, precision=hi)
            - jnp.einsum('tgpn,gqn->tgpq', ab_b_im, c_im, precision=hi))
    kern = kern.at[0].add(jax.vmap(jnp.diag)(d.reshape(g_all, p_ch)))
    j_blocks = g_all // GROUPS_PER_BLOCK

    def blocked(a):
        return a.reshape(a.shape[0], j_blocks, LANES, a.shape[-1]).astype(BF16)

    toeplitz = blocked(jnp.concatenate([kern[::-1], jnp.zeros_like(kern[:1])], axis=0))
    ro_re = c_re[None] * pw_re[1:] - c_im[None] * pw_im[1:]
    ro_im = c_re[None] * pw_im[1:] + c_im[None] * pw_re[1:]
    a_pow = jnp.stack([pw_re[c].reshape(j_blocks, -1), pw_im[c].reshape(j_blocks, -1)], axis=1)
    return (toeplitz, blocked(ab_b_re[::-1]), blocked(ab_b_im[::-1]), blocked(ro_re), blocked(-ro_im), a_pow)


def _expand_block_diag(dst_ref, col0, src_ref, first):
    m = src_ref.shape[-1]
    cols = GROUPS_PER_BLOCK * m
    kk = lax.broadcasted_iota(jnp.int32, (m, cols), 0)
    kc = lax.broadcasted_iota(jnp.int32, (m, cols), 1)
    spread = (kk == (kc & (m - 1))).astype(BF16)
    row = lax.broadcasted_iota(jnp.int32, (LANES, cols), 0)
    col = lax.broadcasted_iota(jnp.int32, (LANES, cols), 1)
    keep = (row >> (S5_GROUP.bit_length() - 1)) == (col >> (m.bit_length() - 1))
    for t in range(S5_CHUNK):
        tiled = _dot(src_ref[first + t], spread)
        dst_ref[t * LANES:(t + 1) * LANES, col0:col0 + cols] = jnp.where(keep, tiled, 0.0).astype(dst_ref.dtype)


def _s5_body(u_ref, tz_ref, sre_ref, sim_ref, ore_ref, oim_ref, ap_ref, y_ref,
             master_ref, ws_ref, wo_ref, s_ref, hp_ref, *, chunks):
    c = S5_CHUNK
    half = hp_ref.shape[1] // 2

    @pl.when(pl.program_id(1) == 0)
    def _():
        _expand_block_diag(master_ref, 0, tz_ref, 1)
        _expand_block_diag(master_ref, LANES, tz_ref, 0)
        _expand_block_diag(ws_ref, 0, sre_ref, 0)
        _expand_block_diag(ws_ref, half, sim_ref, 0)
        _expand_block_diag(wo_ref, 0, ore_ref, 0)
        _expand_block_diag(wo_ref, half, oim_ref, 0)

    u_flat = jnp.concatenate(
        [u_ref[pl.ds(t, chunks, stride=c), :].astype(BF16) for t in range(c)], axis=1)
    s_ref[...] = _dot(u_flat, ws_ref[...])
    a_re = ap_ref[0:1, :]
    a_im = ap_ref[1:2, :]

    def step(k, carry):
        h_re, h_im = carry
        hp_ref[pl.ds(k, 1), :half] = h_re
        hp_ref[pl.ds(k, 1), half:] = h_im
        s = s_ref[pl.ds(k, 1), :]
        return (a_re * h_re - a_im * h_im + s[:, :half], a_re * h_im + a_im * h_re + s[:, half:])

    zero = jnp.zeros((1, half), F32)
    lax.fori_loop(0, chunks, step, (zero, zero), unroll=8)
    h_prev = hp_ref[...].astype(BF16)
    for tp in range(0, c, 2):
        yp = (_dot(u_flat[:, :LANES * (tp + 2)], master_ref[(c - 2 - tp) * LANES:, :])
              + _dot_nt(h_prev, wo_ref[tp * LANES:(tp + 2) * LANES, :]))
        y_ref[pl.ds(tp, chunks, stride=c), :] = yp[:, :LANES]
        y_ref[pl.ds(tp + 1, chunks, stride=c), :] = yp[:, LANES:]


def _s5(u, tables, *, seq):
    t, width = u.shape
    toeplitz, st_re, st_im, ro_re, ro_im, a_pow = tables
    j_blocks = width // LANES
    batch = t // seq
    chunks = seq // S5_CHUNK
    n_state = 2 * GROUPS_PER_BLOCK * st_re.shape[-1]
    rows = S5_CHUNK * LANES
    io = pl.BlockSpec((seq, LANES), lambda j, b: (b, j))

    def table(a):
        return pl.BlockSpec((a.shape[0], None) + a.shape[2:], lambda j, b: (0, j, 0, 0))

    return pl.pallas_call(
        functools.partial(_s5_body, chunks=chunks),
        grid=(j_blocks, batch),
        in_specs=[io, table(toeplitz), table(st_re), table(st_im), table(ro_re), table(ro_im),
                  pl.BlockSpec((None,) + a_pow.shape[1:], lambda j, b: (j, 0, 0))],
        out_specs=io,
        out_shape=jax.ShapeDtypeStruct((t, width), F32),
        scratch_shapes=[pltpu.VMEM((rows, 2 * LANES), BF16),
                        pltpu.VMEM((rows, n_state), BF16),
                        pltpu.VMEM((rows, n_state), BF16),
                        pltpu.VMEM((chunks, n_state), F32),
                        pltpu.VMEM((chunks, n_state), F32)],
        compiler_params=_params(("arbitrary", "arbitrary")),
        name="s5",
    )(u, toeplitz, st_re, st_im, ro_re, ro_im, a_pow)


def _attn_body(q_ref, k_ref, v_ref, o_ref, acc_ref, sp_ref, lb_ref, w_ref, *, tile, blocks):
    qi = pl.program_id(2)
    lane = lax.broadcasted_iota(jnp.int32, (tile, LANES), 1)
    first = lane < SB_HEAD_DIM
    heads = []
    for p in range(blocks):
        q = q_ref[:, p * LANES:(p + 1) * LANES]
        zero_q = jnp.zeros_like(q)
        heads += [(p, jnp.where(first, q, zero_q)), (p, jnp.where(first, zero_q, q))]
    row = lax.broadcasted_iota(jnp.int32, (tile, tile), 0)
    col = lax.broadcasted_iota(jnp.int32, (tile, tile), 1)
    neg_later = jnp.where(row > col, -1.0, 0.0).astype(BF16)
    causal = col < row

    def key_rows(j):
        return pl.ds(pl.multiple_of(j * tile, tile), tile)

    def scores(j, masked):
        for h, (p, qm) in enumerate(heads):
            z = _dot_nt(qm, k_ref[key_rows(j), p * LANES:(p + 1) * LANES])
            sp = jnp.maximum(z, 0.0) + jnp.log(1.0 + jnp.exp2(jnp.abs(z) * -LOG2_E))
            log_beta = z - sp
            if masked:
                sp = jnp.where(causal, sp, 0.0)
                log_beta = jnp.where(causal, log_beta, MASKED_LOG_WEIGHT)
            sp_ref[h] = sp.astype(BF16)
            lb_ref[h] = log_beta

    def weights(carries):
        new = []
        for h in range(len(heads)):
            sp = sp_ref[h]
            r = _dot(sp, neg_later) + carries[h]
            w_ref[h] = jnp.exp(lb_ref[h] + r).astype(BF16)
            new.append(r[:, 0:1] - sp[:, 0:1].astype(F32))
        return tuple(new)

    def accumulate(j):
        for h, (p, _) in enumerate(heads):
            acc_ref[h] += _dot(w_ref[h], v_ref[key_rows(j), p * LANES:(p + 1) * LANES])

    acc_ref[...] = jnp.zeros_like(acc_ref)
    w_ref[...] = jnp.zeros_like(w_ref)
    scores(qi, True)

    def trip(it, carries):
        j = qi - it
        accumulate(jnp.minimum(j + 1, qi))
        carries = weights(carries)
        scores(j - 1, False)
        return carries

    carries = lax.fori_loop(0, qi, trip, tuple(jnp.zeros((tile, 1), F32) for _ in heads))
    accumulate(jnp.minimum(1, qi))
    weights(carries)
    accumulate(0)
    for p in range(blocks):
        o_ref[:, p * LANES:(p + 1) * LANES] = jnp.where(first, acc_ref[2 * p], acc_ref[2 * p + 1]).astype(o_ref.dtype)


def _attn(q, k, v, *, seq, tile=256, blocks=4):
    t, width = q.shape
    batch = t // seq
    groups = width // (blocks * LANES)
    q_tiles = seq // tile
    qo = pl.BlockSpec((tile, blocks * LANES), lambda b, p, i: (b * q_tiles + i, p))
    kv = pl.BlockSpec((seq, blocks * LANES), lambda b, p, i: (b, p))
    return pl.pallas_call(
        functools.partial(_attn_body, tile=tile, blocks=blocks),
        grid=(batch, groups, q_tiles),
        in_specs=[qo, kv, kv],
        out_specs=qo,
        out_shape=jax.ShapeDtypeStruct((t, width), BF16),
        scratch_shapes=[pltpu.VMEM((2 * blocks, tile, LANES), F32),
                        pltpu.VMEM((2 * blocks, tile, tile), BF16),
                        pltpu.VMEM((2 * blocks, tile, tile), F32),
                        pltpu.VMEM((2 * blocks, tile, tile), BF16)],
        compiler_params=_params(("arbitrary", "arbitrary", "arbitrary")),
        name="stickbreak",
    )(q, k, v)


def _mixout_body(x_ref, ya_ref, yb_ref, wglu_ref, wout_ref, o_ref):
    y = ya_ref[...]
    width = y.shape[1]
    z = 0.5 * y * (1.0 + jnp.tanh(math.sqrt(2.0 / math.pi) * (y + 0.044715 * (y * y * y))))
    gate = _dot(z.astype(BF16), wglu_ref[...])
    ya = (z * jax.nn.sigmoid(gate)).astype(BF16)
    o_ref[...] = x_ref[...] + _dot(ya, wout_ref[:width, :]) + _dot(yb_ref[...], wout_ref[width:, :])


def _mixout(x, ya, yb, wglu, wout, *, tm=512):
    t, d = x.shape
    width = ya.shape[1]
    row = pl.BlockSpec((tm, d), lambda i: (i, 0))
    half = pl.BlockSpec((tm, width), lambda i: (i, 0))
    return pl.pallas_call(
        _mixout_body,
        grid=(t // tm,),
        in_specs=[row, half, half, _resident(wglu.shape), _resident(wout.shape)],
        out_specs=row,
        out_shape=jax.ShapeDtypeStruct((t, d), F32),
        compiler_params=_params(("arbitrary",)),
        name="mixout0",
    )(x, ya, yb, wglu, wout)


def _conv_body(x_ref, g_ref, win_ref, cw_ref, wout_ref, o_ref, prev_ref, *, tiles_per_seq, ch):
    i = pl.program_id(0)

    @pl.when(i % tiles_per_seq == 0)
    def _():
        prev_ref[...] = jnp.zeros_like(prev_ref)

    x = x_ref[...]
    tm = x.shape[0]
    h = _rms(x, g_ref[...]).astype(BF16)
    p = _dot(h, win_ref[...])
    b_gate = p[:, :ch]
    cv = p[:, ch:2 * ch] * p[:, 2 * ch:]
    prev = prev_ref[...]
    row = lax.broadcasted_iota(jnp.int32, (tm, 1), 0)
    back1 = jnp.where(row == 0, prev[7:8, :], pltpu.roll(cv, 1, 0))
    back2 = jnp.where(row == 0, prev[6:7, :], jnp.where(row == 1, prev[7:8, :], pltpu.roll(cv, 2, 0)))
    y = cw_ref[0:1, :] * back2 + cw_ref[1:2, :] * back1 + cw_ref[2:3, :] * cv
    prev_ref[...] = cv[tm - 8:, :]
    o_ref[...] = x + _dot((b_gate * y).astype(BF16), wout_ref[...])


def _conv_mixer(x, g, win, cw, wout, *, seq, tm=512):
    t, d = x.shape
    ch = wout.shape[0]
    row = pl.BlockSpec((tm, d), lambda i: (i, 0))
    return pl.pallas_call(
        functools.partial(_conv_body, tiles_per_seq=seq // tm, ch=ch),
        grid=(t // tm,),
        in_specs=[row, _resident((1, d)), _resident(win.shape), _resident(cw.shape), _resident(wout.shape)],
        out_specs=row,
        out_shape=jax.ShapeDtypeStruct((t, d), F32),
        scratch_shapes=[pltpu.VMEM((8, ch), F32)],
        compiler_params=_params(("arbitrary",)),
        name="conv_mixer",
    )(x, g, win, cw, wout)


def _ffn_weights(wg, wu, wd):
    return _cast_pad(wg, axis=2), _cast_pad(wu, axis=2), _cast_pad(wd, axis=1)


def kernel(x, ffn1_norm, ffn1_w_gate, ffn1_w_up, ffn1_w_down, mix_norm, ffn2_norm, ffn2_w_gate, ffn2_w_up,
           ffn2_w_down, ab_w_in, s5_lambda_re, s5_lambda_im, s5_log_dt, s5_b_re, s5_b_im, s5_c_re, s5_c_im, s5_d,
           s5_w_glu, ab_w_out, sc_w_in, sc_conv_w, sc_w_out, final_norm):
    batch, seq, d = x.shape
    depth = ffn1_norm.shape[0]
    xt = x.reshape(batch * seq, d)
    ffn1 = _ffn_weights(ffn1_w_gate, ffn1_w_up, ffn1_w_down)
    ffn2 = _ffn_weights(ffn2_w_gate, ffn2_w_up, ffn2_w_down)
    for layer in range(depth):
        xt = _ffn(xt, ffn1_norm[layer][None], ffn1, layer)
        g_mix = mix_norm[layer][None]
        if layer % 2 == 0:
            e = layer // 2
            u, q, k, v = _proj0(xt, g_mix, ab_w_in[e].astype(BF16))
            tables = _s5_tables(s5_lambda_re[e], s5_lambda_im[e], s5_log_dt[e], s5_b_re[e], s5_b_im[e],
                                s5_c_re[e], s5_c_im[e], s5_d[e])
            ya = _s5(u, tables, seq=seq)
            yb = _attn(q, k, v, seq=seq)
            xt = _mixout(xt, ya, yb, s5_w_glu[e].astype(BF16), ab_w_out[e].astype(BF16))
        else:
            o = layer // 2
            xt = _conv_mixer(xt, g_mix, sc_w_in[o].astype(BF16), sc_conv_w[o], sc_w_out[o].astype(BF16), seq=seq)
        last = layer == depth - 1
        xt = _ffn(xt, ffn2_norm[layer][None], ffn2, layer, final_g=final_norm[None] if last else None)
    return xt.reshape(batch, seq, d)
```

```python
import functools
import math

import jax
import jax.numpy as jnp
from jax import lax
from jax.experimental import pallas as pl
from jax.experimental.pallas import tpu as pltpu

F32 = jnp.float32
BF16 = jnp.bfloat16

EPS = 1e-6
LANES = 128
S5_GROUP = 16
S5_STATE = 64
S5_CHUNK = 16
GROUPS_PER_BLOCK = LANES // S5_GROUP
SB_HEAD_DIM = 64
LOG2_E = math.log2(math.e)
MASKED_LOG_WEIGHT = -1e30
CONV_K = 3
VMEM_LIMIT_BYTES = 56 * 1024 * 1024


def _params(semantics):
    return pltpu.CompilerParams(dimension_semantics=semantics, vmem_limit_bytes=VMEM_LIMIT_BYTES)


def _resident(shape):
    zeros = (0,) * len(shape)
    return pl.BlockSpec(shape, lambda *_: zeros, pipeline_mode=pl.Buffered(1))


def _rms(x, g):
    r = lax.rsqrt(jnp.mean(x * x, axis=-1, keepdims=True) + EPS)
    return x * r * g


def _dot(a, b):
    return jnp.dot(a, b, preferred_element_type=F32)


def _dot_nt(a, b):
    return lax.dot_general(a, b, (((1,), (1,)), ((), ())), preferred_element_type=F32)


def _mix_out(ya_ref, yb_ref, wglu_ref, wout_ref):
    y = ya_ref[...]
    width = y.shape[1]
    z = 0.5 * y * (1.0 + jnp.tanh(math.sqrt(2.0 / math.pi) * (y + 0.044715 * (y * y * y))))
    gate = _dot(z.astype(BF16), wglu_ref[...])
    ya = (z * jax.nn.sigmoid(gate)).astype(BF16)
    return _dot(ya, wout_ref[:width, :]) + _dot(yb_ref[...], wout_ref[width:, :])


def _ffn_body(*refs, mixer, final_norm):
    refs = list(refs)
    x_ref = refs[0]
    pos = 1
    if mixer:
        mix_refs = refs[1:5]
        pos = 5
    g_ref, wg_ref, wu_ref, wd_ref = refs[pos:pos + 4]
    pos += 4
    if final_norm:
        fg_ref = refs[pos]
        pos += 1
    (o_ref,) = refs[pos:]
    x = x_ref[...]
    if mixer:
        x = x + _mix_out(*mix_refs)
    h = _rms(x, g_ref[...]).astype(BF16)
    gate = _dot(h, wg_ref[...])
    up = _dot(h, wu_ref[...])
    act = (gate * jax.nn.sigmoid(gate) * up).astype(BF16)
    y = x + 0.5 * _dot(act, wd_ref[...])
    if final_norm:
        y = _rms(y, fg_ref[...])
    o_ref[...] = y


def _ffn(x, g, weights, layer, *, tm=512, mixer=None, final_g=None):
    t, d = x.shape
    wg, wu, wd = weights
    f = wg.shape[2]

    def rows(width):
        return pl.BlockSpec((tm, width), lambda i: (i, 0))

    def of_layer(shape):
        return pl.BlockSpec((None,) + shape, lambda i: (layer, 0, 0), pipeline_mode=pl.Buffered(1))

    in_specs = [rows(d)]
    args = [x]
    if mixer is not None:
        ya, yb, wglu, wout = mixer
        in_specs += [rows(ya.shape[1]), rows(yb.shape[1]), _resident(wglu.shape), _resident(wout.shape)]
        args += [ya, yb, wglu, wout]
    in_specs += [_resident((1, d)), of_layer((d, f)), of_layer((d, f)), of_layer((f, d))]
    args += [g, wg, wu, wd]
    if final_g is not None:
        in_specs.append(_resident((1, d)))
        args.append(final_g)
    return pl.pallas_call(
        functools.partial(_ffn_body, mixer=mixer is not None, final_norm=final_g is not None),
        grid=(t // tm,),
        in_specs=in_specs,
        out_specs=rows(d),
        out_shape=jax.ShapeDtypeStruct((t, d), F32),
        compiler_params=_params(("arbitrary",)),
        name="ffn",
    )(*args)


def _proj0_body(x_ref, g_ref, w_ref, u_ref, q_ref, k_ref, v_ref, *, width, scale):
    h = _rms(x_ref[...], g_ref[...]).astype(BF16)
    p = _dot(h, w_ref[...])
    u_ref[...] = p[:, :width]
    q_ref[...] = (p[:, width:2 * width] * scale).astype(BF16)
    k_ref[...] = p[:, 2 * width:3 * width].astype(BF16)
    v_ref[...] = p[:, 3 * width:].astype(BF16)


def _proj0(x, g, w, *, tm=512):
    t, d = x.shape
    width = w.shape[1] // 4
    row = pl.BlockSpec((tm, d), lambda i: (i, 0))
    out = pl.BlockSpec((tm, width), lambda i: (i, 0))
    return pl.pallas_call(
        functools.partial(_proj0_body, width=width, scale=1.0 / math.sqrt(SB_HEAD_DIM)),
        grid=(t // tm,),
        in_specs=[row, _resident((1, d)), _resident(w.shape)],
        out_specs=[out, out, out, out],
        out_shape=[jax.ShapeDtypeStruct((t, width), F32)] + [jax.ShapeDtypeStruct((t, width), BF16)] * 3,
        compiler_params=_params(("arbitrary",)),
        name="proj0",
    )(x, g, w)


def _s5_tables(lam_re, lam_im, log_dt, b_re, b_im, c_re, c_im, d):
    hi = lax.Precision.HIGHEST
    g_all, n_st = lam_re.shape
    p_ch = b_re.shape[-1]
    c = S5_CHUNK
    dt = jnp.exp(log_dt)[:, None]
    mag = jnp.exp(lam_re * dt)
    ab_re = mag * jnp.cos(lam_im * dt)
    ab_im = mag * jnp.sin(lam_im * dt)
    den = lam_re * lam_re + lam_im * lam_im
    nr = ab_re - 1.0
    coef_re = (nr * lam_re + ab_im * lam_im) / den
    coef_im = (ab_im * lam_re - nr * lam_im) / den
    b_re_t = jnp.swapaxes(b_re, -1, -2)
    b_im_t = jnp.swapaxes(b_im, -1, -2)
    bb_re = coef_re[:, None, :] * b_re_t - coef_im[:, None, :] * b_im_t
    bb_im = coef_re[:, None, :] * b_im_t + coef_im[:, None, :] * b_re_t
    pw_re = [jnp.ones_like(ab_re)]
    pw_im = [jnp.zeros_like(ab_im)]
    for _ in range(c):
        pr, pi = pw_re[-1], pw_im[-1]
        pw_re.append(pr * ab_re - pi * ab_im)
        pw_im.append(pr * ab_im + pi * ab_re)
    pw_re = jnp.stack(pw_re)[:, :, None, :]
    pw_im = jnp.stack(pw_im)[:, :, None, :]
    ab_b_re = pw_re[:c] * bb_re - pw_im[:c] * bb_im
    ab_b_im = pw_re[:c] * bb_im + pw_im[:c] * bb_re
    kern = (jnp.einsum('tgpn,gqn->tgpq', ab_b_re, c_re, precision=hi)
            - jnp.einsum('tgpn,gqn->tgpq', ab_b_im, c_im, precision=hi))
    kern = kern.at[0].add(jax.vmap(jnp.diag)(d.reshape(g_all, p_ch)))
    j_blocks = g_all // GROUPS_PER_BLOCK

    def blocked(a):
        return a.reshape(a.shape[0], j_blocks, LANES, a.shape[-1]).astype(BF16)

    toeplitz = blocked(jnp.concatenate([kern[::-1], jnp.zeros_like(kern[:1])], axis=0))
    ro_re = c_re[None] * pw_re[1:] - c_im[None] * pw_im[1:]
    ro_im = c_re[None] * pw_im[1:] + c_im[None] * pw_re[1:]
    a_pow = jnp.stack([pw_re[c].reshape(j_blocks, -1), pw_im[c].reshape(j_blocks, -1)], axis=1)
    return (toeplitz, blocked(ab_b_re[::-1]), blocked(ab_b_im[::-1]), blocked(ro_re), blocked(-ro_im), a_pow)


def _expand_block_diag(dst_ref, col0, src_ref, first):
    m = src_ref.shape[-1]
    cols = GROUPS_PER_BLOCK * m
    kk = lax.broadcasted_iota(jnp.int32, (m, cols), 0)
    kc = lax.broadcasted_iota(jnp.int32, (m, cols), 1)
    spread = (kk == (kc & (m - 1))).astype(BF16)
    row = lax.broadcasted_iota(jnp.int32, (LANES, cols), 0)
    col = lax.broadcasted_iota(jnp.int32, (LANES, cols), 1)
    keep = (row >> (S5_GROUP.bit_length() - 1)) == (col >> (m.bit_length() - 1))
    for t in range(S5_CHUNK):
        tiled = _dot(src_ref[first + t], spread)
        dst_ref[t * LANES:(t + 1) * LANES, col0:col0 + cols] = jnp.where(keep, tiled, 0.0).astype(dst_ref.dtype)


def _s5_body(u_ref, tz_ref, sre_ref, sim_ref, ore_ref, oim_ref, ap_ref, y_ref,
             master_ref, ws_ref, wo_ref, s_ref, hp_ref, *, chunks):
    c = S5_CHUNK
    half = hp_ref.shape[1] // 2

    @pl.when(pl.program_id(1) == 0)
    def _():
        _expand_block_diag(master_ref, 0, tz_ref, 1)
        _expand_block_diag(master_ref, LANES, tz_ref, 0)
        _expand_block_diag(ws_ref, 0, sre_ref, 0)
        _expand_block_diag(ws_ref, half, sim_ref, 0)
        _expand_block_diag(wo_ref, 0, ore_ref, 0)
        _expand_block_diag(wo_ref, half, oim_ref, 0)

    u_flat = jnp.concatenate(
        [u_ref[pl.ds(t, chunks, stride=c), :].astype(BF16) for t in range(c)], axis=1)
    s_ref[...] = _dot(u_flat, ws_ref[...])
    a_re = ap_ref[0:1, :]
    a_im = ap_ref[1:2, :]

    def step(k, carry):
        h_re, h_im = carry
        hp_ref[pl.ds(k, 1), :half] = h_re
        hp_ref[pl.ds(k, 1), half:] = h_im
        s = s_ref[pl.ds(k, 1), :]
        return (a_re * h_re - a_im * h_im + s[:, :half], a_re * h_im + a_im * h_re + s[:, half:])

    zero = jnp.zeros((1, half), F32)
    lax.fori_loop(0, chunks, step, (zero, zero), unroll=8)
    h_prev = hp_ref[...].astype(BF16)
    for tp in range(0, c, 2):
        yp = (_dot(u_flat[:, :LANES * (tp + 2)], master_ref[(c - 2 - tp) * LANES:, :])
              + _dot_nt(h_prev, wo_ref[tp * LANES:(tp + 2) * LANES, :]))
        y_ref[pl.ds(tp, chunks, stride=c), :] = yp[:, :LANES]
        y_ref[pl.ds(tp + 1, chunks, stride=c), :] = yp[:, LANES:]


def _s5(u, tables, *, seq):
    t, width = u.shape
    toeplitz, st_re, st_im, ro_re, ro_im, a_pow = tables
    j_blocks = width // LANES
    batch = t // seq
    chunks = seq // S5_CHUNK
    n_state = 2 * GROUPS_PER_BLOCK * st_re.shape[-1]
    rows = S5_CHUNK * LANES
    io = pl.BlockSpec((seq, LANES), lambda j, b: (b, j))

    def table(a):
        return pl.BlockSpec((a.shape[0], None) + a.shape[2:], lambda j, b: (0, j, 0, 0))

    return pl.pallas_call(
        functools.partial(_s5_body, chunks=chunks),
        grid=(j_blocks, batch),
        in_specs=[io, table(toeplitz), table(st_re), table(st_im), table(ro_re), table(ro_im),
                  pl.BlockSpec((None,) + a_pow.shape[1:], lambda j, b: (j, 0, 0))],
        out_specs=io,
        out_shape=jax.ShapeDtypeStruct((t, width), F32),
        scratch_shapes=[pltpu.VMEM((rows, 2 * LANES), BF16),
                        pltpu.VMEM((rows, n_state), BF16),
                        pltpu.VMEM((rows, n_state), BF16),
                        pltpu.VMEM((chunks, n_state), F32),
                        pltpu.VMEM((chunks, n_state), F32)],
        compiler_params=_params(("arbitrary", "arbitrary")),
        name="s5",
    )(u, toeplitz, st_re, st_im, ro_re, ro_im, a_pow)


def _attn_body(q_ref, k_ref, v_ref, o_ref, acc_ref, sp_ref, lb_ref, w_ref, *, tile, blocks):
    qi = pl.program_id(2)
    lane = lax.broadcasted_iota(jnp.int32, (tile, LANES), 1)
    first = lane < SB_HEAD_DIM
    heads = []
    for p in range(blocks):
        q = q_ref[:, p * LANES:(p + 1) * LANES]
        zero_q = jnp.zeros_like(q)
        heads += [(p, jnp.where(first, q, zero_q)), (p, jnp.where(first, zero_q, q))]
    row = lax.broadcasted_iota(jnp.int32, (tile, tile), 0)
    col = lax.broadcasted_iota(jnp.int32, (tile, tile), 1)
    neg_later = jnp.where(row > col, -1.0, 0.0).astype(BF16)
    causal = col < row

    def key_rows(j):
        return pl.ds(pl.multiple_of(j * tile, tile), tile)

    def scores(j, masked):
        for h, (p, qm) in enumerate(heads):
            z = _dot_nt(qm, k_ref[key_rows(j), p * LANES:(p + 1) * LANES])
            sp = jnp.maximum(z, 0.0) + jnp.log(1.0 + jnp.exp2(jnp.abs(z) * -LOG2_E))
            log_beta = z - sp
            if masked:
                sp = jnp.where(causal, sp, 0.0)
                log_beta = jnp.where(causal, log_beta, MASKED_LOG_WEIGHT)
            sp_ref[h] = sp.astype(BF16)
            lb_ref[h] = log_beta

    def weights(carries):
        new = []
        for h in range(len(heads)):
            sp = sp_ref[h]
            r = _dot(sp, neg_later) + carries[h]
            w_ref[h] = jnp.exp(lb_ref[h] + r).astype(BF16)
            new.append(r[:, 0:1] - sp[:, 0:1].astype(F32))
        return tuple(new)

    def accumulate(j):
        for h, (p, _) in enumerate(heads):
            acc_ref[h] += _dot(w_ref[h], v_ref[key_rows(j), p * LANES:(p + 1) * LANES])

    acc_ref[...] = jnp.zeros_like(acc_ref)
    w_ref[...] = jnp.zeros_like(w_ref)
    scores(qi, True)

    def trip(it, carries):
        j = qi - it
        accumulate(jnp.minimum(j + 1, qi))
        carries = weights(carries)
        scores(j - 1, False)
        return carries

    carries = lax.fori_loop(0, qi, trip, tuple(jnp.zeros((tile, 1), F32) for _ in heads))
    accumulate(jnp.minimum(1, qi))
    weights(carries)
    accumulate(0)
    for p in range(blocks):
        o_ref[:, p * LANES:(p + 1) * LANES] = jnp.where(first, acc_ref[2 * p], acc_ref[2 * p + 1]).astype(o_ref.dtype)


def _attn(q, k, v, *, seq, tile=256, blocks=4):
    t, width = q.shape
    batch = t // seq
    groups = width // (blocks * LANES)
    q_tiles = seq // tile
    qo = pl.BlockSpec((tile, blocks * LANES), lambda b, p, i: (b * q_tiles + i, p))
    kv = pl.BlockSpec((seq, blocks * LANES), lambda b, p, i: (b, p))
    return pl.pallas_call(
        functools.partial(_attn_body, tile=tile, blocks=blocks),
        grid=(batch, groups, q_tiles),
        in_specs=[qo, kv, kv],
        out_specs=qo,
        out_shape=jax.ShapeDtypeStruct((t, width), BF16),
        scratch_shapes=[pltpu.VMEM((2 * blocks, tile, LANES), F32),
                        pltpu.VMEM((2 * blocks, tile, tile), BF16),
                        pltpu.VMEM((2 * blocks, tile, tile), F32),
                        pltpu.VMEM((2 * blocks, tile, tile), BF16)],
        compiler_params=_params(("arbitrary", "arbitrary", "arbitrary")),
        name="stickbreak",
    )(q, k, v)


def _conv_body(x_ref, g_ref, win_ref, cw_ref, wout_ref, o_ref, prev_ref, *, tiles_per_seq, ch):
    i = pl.program_id(0)

    @pl.when(i % tiles_per_seq == 0)
    def _():
        prev_ref[...] = jnp.zeros_like(prev_ref)

    x = x_ref[...]
    tm = x.shape[0]
    h = _rms(x, g_ref[...]).astype(BF16)
    p = _dot(h, win_ref[...])
    b_gate = p[:, :ch]
    cv = p[:, ch:2 * ch] * p[:, 2 * ch:]
    prev = prev_ref[...]
    row = lax.broadcasted_iota(jnp.int32, (tm, 1), 0)
    back1 = jnp.where(row == 0, prev[7:8, :], pltpu.roll(cv, 1, 0))
    back2 = jnp.where(row == 0, prev[6:7, :], jnp.where(row == 1, prev[7:8, :], pltpu.roll(cv, 2, 0)))
    y = cw_ref[0:1, :] * back2 + cw_ref[1:2, :] * back1 + cw_ref[2:3, :] * cv
    prev_ref[...] = cv[tm - 8:, :]
    o_ref[...] = x + _dot((b_gate * y).astype(BF16), wout_ref[...])


def _conv_mixer(x, g, win, cw, wout, *, seq, tm=512):
    t, d = x.shape
    ch = wout.shape[0]
    row = pl.BlockSpec((tm, d), lambda i: (i, 0))
    return pl.pallas_call(
        functools.partial(_conv_body, tiles_per_seq=seq // tm, ch=ch),
        grid=(t // tm,),
        in_specs=[row, _resident((1, d)), _resident(win.shape), _resident(cw.shape), _resident(wout.shape)],
        out_specs=row,
        out_shape=jax.ShapeDtypeStruct((t, d), F32),
        scratch_shapes=[pltpu.VMEM((8, ch), F32)],
        compiler_params=_params(("arbitrary",)),
        name="conv_mixer",
    )(x, g, win, cw, wout)


def _ffn_weights(wg, wu, wd):
    return wg.astype(BF16), wu.astype(BF16), wd.astype(BF16)


def kernel(x, ffn1_norm, ffn1_w_gate, ffn1_w_up, ffn1_w_down, mix_norm, ffn2_norm, ffn2_w_gate, ffn2_w_up,
           ffn2_w_down, ab_w_in, s5_lambda_re, s5_lambda_im, s5_log_dt, s5_b_re, s5_b_im, s5_c_re, s5_c_im, s5_d,
           s5_w_glu, ab_w_out, sc_w_in, sc_conv_w, sc_w_out, final_norm):
    batch, seq, d = x.shape
    depth = ffn1_norm.shape[0]
    xt = x.reshape(batch * seq, d)
    ffn1 = _ffn_weights(ffn1_w_gate, ffn1_w_up, ffn1_w_down)
    ffn2 = _ffn_weights(ffn2_w_gate, ffn2_w_up, ffn2_w_down)
    for layer in range(depth):
        xt = _ffn(xt, ffn1_norm[layer][None], ffn1, layer)
        g_mix = mix_norm[layer][None]
        mixer = None
        if layer % 2 == 0:
            e = layer // 2
            u, q, k, v = _proj0(xt, g_mix, ab_w_in[e].astype(BF16))
            tables = _s5_tables(s5_lambda_re[e], s5_lambda_im[e], s5_log_dt[e], s5_b_re[e], s5_b_im[e],
                                s5_c_re[e], s5_c_im[e], s5_d[e])
            ya = _s5(u, tables, seq=seq)
            yb = _attn(q, k, v, seq=seq)
            mixer = (ya, yb, s5_w_glu[e].astype(BF16), ab_w_out[e].astype(BF16))
        else:
            o = layer // 2
            xt = _conv_mixer(xt, g_mix, sc_w_in[o].astype(BF16), sc_conv_w[o], sc_w_out[o].astype(BF16), seq=seq)
        last = layer == depth - 1
        xt = _ffn(xt, ffn2_norm[layer][None], ffn2, layer, mixer=mixer, final_g=final_norm[None] if last else None)
    return xt.reshape(batch, seq, d)
```
